```python
import jax, jax.numpy as jnp
from jax import lax
import numpy as np

D_MODEL = 2048
BATCH = 1
SEQ = 8192
DEPTH = 2
DEC_BATCH = 2
DEC_SEQ = 8192
PAST_LEN = 128

HEAD_DIM = 128
A_Q_HEADS = 6
A_KV_HEADS = 2
A_GROUP = A_Q_HEADS // A_KV_HEADS
A_WINDOW = 128
A_BLOCK = 128
B_PATTERNS = ((128, 1), (512, 4), (2048, 16))
B_HEADS_PER_GROUP = 2
B_HEADS = B_HEADS_PER_GROUP * len(B_PATTERNS)
B_BLOCK = 64
C_HEADS = 4
GRID_W = 64
NA_ROWS = 8
NA_COLS = 16
NA_QCOLS = 16
NA_KCOLS = 2 * NA_COLS
MIX_WIDTH = (A_Q_HEADS + B_HEADS + C_HEADS) * HEAD_DIM
IN_WIDTH = (A_Q_HEADS + 2 * A_KV_HEADS + 3 * B_HEADS + 3 * C_HEADS) * HEAD_DIM
D_FF = -(-8 * D_MODEL // (3 * 256)) * 256
ROT_DIM = HEAD_DIM // 4
ROPE_THETA = 500000.0
EPS = 1e-6
NEG = -1e30

kernel_name = 'hybrid_parallel_local_dilated_neighbourhood_encoder'


def rms_norm(x, g):
    x32 = x.astype(jnp.float32)
    y = x32 * lax.rsqrt(jnp.mean(x32 * x32, axis=-1, keepdims=True) + EPS) * g.astype(jnp.float32)
    return y.astype(x.dtype)


def partial_rope(x):
    seq = x.shape[1]
    half = ROT_DIM // 2
    inv = jnp.asarray((ROPE_THETA ** (-np.arange(0, ROT_DIM, 2, dtype=np.float32) / ROT_DIM)).astype(np.float32))
    ang = jnp.arange(seq, dtype=jnp.float32)[:, None] * inv[None, :]
    shape = (1, seq) + (1,) * (x.ndim - 3) + (half,)
    cos = jnp.cos(ang).reshape(shape)
    sin = jnp.sin(ang).reshape(shape)
    xr = x[..., :ROT_DIM].astype(jnp.float32)
    x1, x2 = xr[..., :half], xr[..., half:]
    rot = jnp.concatenate([x1 * cos - x2 * sin, x2 * cos + x1 * sin], axis=-1)
    return jnp.concatenate([rot.astype(x.dtype), x[..., ROT_DIM:]], axis=-1)


def banded_attention(q, k, v, half_window, block, sink=None, with_lse=False):
    n, length, hk, g, hd = q.shape
    nb = -(-length // block)
    tail = nb * block - length
    qb = jnp.pad(q, ((0, 0), (0, tail), (0, 0), (0, 0), (0, 0))).reshape(n, nb, block, hk, g, hd)
    pad_kv = ((0, 0), (block, block + tail), (0, 0), (0, 0))

    def windows(t):
        tp = jnp.pad(t, pad_kv).reshape(n, nb + 2, block, hk, hd)
        return jnp.concatenate([tp[:, :-2], tp[:, 1:-1], tp[:, 2:]], axis=2)

    kw, vw = windows(k), windows(v)
    qpos = np.arange(nb * block).reshape(nb, block)
    kpos = (np.arange(nb)[:, None] - 1) * block + np.arange(3 * block)[None, :]
    valid = (np.abs(qpos[:, :, None] - kpos[:, None, :]) <= half_window) & ((kpos >= 0) & (kpos < length))[:, None, :]
    s = jnp.einsum('nbqhgd,nbkhd->nbhgqk', qb, kw, preferred_element_type=jnp.float32) * (hd ** -0.5)
    s = jnp.where(jnp.asarray(valid)[None, :, None, None], s, NEG)
    m = s.max(axis=-1)
    if sink is not None:
        sink = sink.astype(jnp.float32)[None, None, :, :, None]
        m = jnp.maximum(m, sink)
    p = jnp.exp(s - m[..., None])
    denom = p.sum(axis=-1)
    if sink is not None:
        denom = denom + jnp.exp(sink - m)
    o = jnp.einsum('nbhgqk,nbkhd->nbqhgd', p.astype(v.dtype), vw, preferred_element_type=jnp.float32)
    o = o / jnp.moveaxis(denom, 4, 2)[..., None]
    o = o.astype(q.dtype).reshape(n, nb * block, hk, g, hd)[:, :length]
    if with_lse:
        lse = jnp.moveaxis(m + jnp.log(denom), 4, 2).reshape(n, nb * block, hk, g)[:, :length]
        return o, lse
    return o


def to_residue(x, d):
    b, s = x.shape[:2]
    rest = x.shape[2:]
    x = jnp.moveaxis(x.reshape((b, s // d, d) + rest), 2, 1)
    return x.reshape((b * d, s // d) + rest)


def from_residue(x, d, b):
    sd = x.shape[1]
    rest = x.shape[2:]
    x = jnp.moveaxis(x.reshape((b, d, sd) + rest), 1, 2)
    return x.reshape((b, sd * d) + rest)


def dilated_attention(q, k, v):
    bsz, seq = q.shape[:2]
    outs, lses = [], []
    for gi, (window, dil) in enumerate(B_PATTERNS):
        sl = slice(gi * B_HEADS_PER_GROUP, (gi + 1) * B_HEADS_PER_GROUP)
        qg = to_residue(q[:, :, sl, None], dil)
        kg = to_residue(k[:, :, sl], dil)
        vg = to_residue(v[:, :, sl], dil)
        o, lse = banded_attention(qg, kg, vg, window // (2 * dil), B_BLOCK, with_lse=True)
        outs.append(from_residue(o[:, :, :, 0], dil, bsz))
        lses.append(from_residue(lse[:, :, :, 0], dil, bsz))
    alpha = jax.nn.softmax(jnp.stack(lses, axis=0), axis=0)
    o = jnp.concatenate([og * alpha[gi][..., None].astype(og.dtype) for gi, og in enumerate(outs)], axis=2)
    return o.reshape(bsz, seq, B_HEADS * HEAD_DIM)


def neighbourhood_attention(q, k, v, rpb):
    bsz, seq, h, hd = q.shape
    rows = seq // GRID_W
    kr = min(NA_ROWS, rows)
    ncb = GRID_W // NA_QCOLS
    r = np.arange(rows)
    key_rows = np.clip(r - kr // 2, 0, rows - kr)[:, None] + np.arange(kr)[None, :]
    cb = np.arange(ncb)
    key_cols = np.clip(cb * NA_QCOLS - NA_COLS // 2, 0, GRID_W - NA_KCOLS)[:, None] + np.arange(NA_KCOLS)[None, :]
    nk = kr * NA_KCOLS
    krow = np.broadcast_to(key_rows[:, :, None], (rows, kr, NA_KCOLS)).reshape(rows, nk)
    kcol = np.broadcast_to(key_cols[:, None, :], (ncb, kr, NA_KCOLS)).reshape(ncb, nk)
    idx = krow[:, None, :] * GRID_W + kcol[None, :, :]
    qcol = cb[:, None] * NA_QCOLS + np.arange(NA_QCOLS)[None, :]
    cstart = np.clip(qcol - NA_COLS // 2, 0, GRID_W - NA_COLS)
    col_valid = (kcol[:, None, :] >= cstart[:, :, None]) & (kcol[:, None, :] < cstart[:, :, None] + NA_COLS)
    rel_row = krow - r[:, None] + NA_ROWS - 1
    rel_col = np.clip(kcol[:, None, :] - qcol[:, :, None] + NA_COLS - 1, 0, 2 * NA_COLS - 2)
    bias = rpb.astype(jnp.float32)[:, rel_row[:, None, None, :], rel_col[None]]
    kg = k[:, idx]
    vg = v[:, idx]
    qb = q.reshape(bsz, rows, ncb, NA_QCOLS, h, hd)
    s = jnp.einsum('bnjqhd,bnjkhd->bnjhqk', qb, kg, preferred_element_type=jnp.float32) * (hd ** -0.5)
    s = s + jnp.transpose(bias, (1, 2, 0, 3, 4))[None]
    s = jnp.where(jnp.asarray(col_valid)[None, None, :, None], s, NEG)
    p = jax.nn.softmax(s, axis=-1)
    o = jnp.einsum('bnjhqk,bnjkhd->bnjqhd', p.astype(v.dtype), vg)
    return o.reshape(bsz, seq, h * hd)


def encoder_layer(x, norm_mix, w_in, qk_norm, sink_a, rpb_c, w_out, norm_ffn, w_gate, w_up, w_down):
    bsz, seq, _ = x.shape
    hn = rms_norm(x, norm_mix)
    proj = jnp.einsum('bsd,de->bse', hn, w_in)
    sizes = [A_Q_HEADS, A_KV_HEADS, A_KV_HEADS, B_HEADS, B_HEADS, B_HEADS, C_HEADS, C_HEADS, C_HEADS]
    cuts, acc = [], 0
    for sz in sizes[:-1]:
        acc += sz * HEAD_DIM
        cuts.append(acc)
    qa, ka, va, qb, kb, vb, qc, kc, vc = jnp.split(proj, cuts, axis=-1)
    qa = partial_rope(rms_norm(qa.reshape(bsz, seq, A_KV_HEADS, A_GROUP, HEAD_DIM), qk_norm[0, 0]))
    ka = partial_rope(rms_norm(ka.reshape(bsz, seq, A_KV_HEADS, HEAD_DIM), qk_norm[0, 1]))
    va = va.reshape(bsz, seq, A_KV_HEADS, HEAD_DIM)
    oa = banded_attention(qa, ka, va, A_WINDOW, A_BLOCK, sink=sink_a.reshape(A_KV_HEADS, A_GROUP))
    oa = oa.reshape(bsz, seq, A_Q_HEADS * HEAD_DIM)
    qb = partial_rope(rms_norm(qb.reshape(bsz, seq, B_HEADS, HEAD_DIM), qk_norm[1, 0]))
    kb = partial_rope(rms_norm(kb.reshape(bsz, seq, B_HEADS, HEAD_DIM), qk_norm[1, 1]))
    ob = dilated_attention(qb, kb, vb.reshape(bsz, seq, B_HEADS, HEAD_DIM))
    qc = rms_norm(qc.reshape(bsz, seq, C_HEADS, HEAD_DIM), qk_norm[2, 0])
    kc = rms_norm(kc.reshape(bsz, seq, C_HEADS, HEAD_DIM), qk_norm[2, 1])
    oc = neighbourhood_attention(qc, kc, vc.reshape(bsz, seq, C_HEADS, HEAD_DIM), rpb_c)
    mixed = jnp.concatenate([oa, ob, oc], axis=-1)
    x = x + jnp.einsum('bse,ed->bsd', mixed, w_out)
    hn = rms_norm(x, norm_ffn)
    ff = jax.nn.silu(jnp.einsum('bsd,df->bsf', hn, w_gate)) * jnp.einsum('bsd,df->bsf', hn, w_up)
    return x + jnp.einsum('bsf,fd->bsd', ff, w_down)


def trunk(x, norm_mix, w_in, qk_norm, sink_a, rpb_c, w_out, norm_ffn, w_gate, w_up, w_down):
    for l in range(DEPTH):
        x = encoder_layer(x, norm_mix[l], w_in[l], qk_norm[l], sink_a[l], rpb_c[l], w_out[l],
                          norm_ffn[l], w_gate[l], w_up[l], w_down[l])
    return x


def setup_inputs(seed: int = 0) -> dict:
    key = jax.random.key(seed)
    ks = jax.random.split(key, 12)
    f32 = jnp.float32
    nrm = lambda k, shape, scale: jax.random.normal(k, shape, f32) * scale
    return {
        'x_prompt': nrm(ks[0], (BATCH, SEQ, D_MODEL), 1.0),
        'x_sample': nrm(ks[1], (DEC_BATCH, DEC_SEQ, D_MODEL), 1.0),
        'norm_mix': 1.0 + nrm(ks[2], (DEPTH, D_MODEL), 0.02),
        'w_in': nrm(ks[3], (DEPTH, D_MODEL, IN_WIDTH), D_MODEL ** -0.5),
        'qk_norm': 1.0 + nrm(ks[4], (DEPTH, 3, 2, HEAD_DIM), 0.02),
        'sink_a': nrm(ks[5], (DEPTH, A_Q_HEADS), 0.5),
        'rpb_c': nrm(ks[6], (DEPTH, C_HEADS, 2 * NA_ROWS - 1, 2 * NA_COLS - 1), 0.1),
        'w_out': nrm(ks[7], (DEPTH, MIX_WIDTH, D_MODEL), MIX_WIDTH ** -0.5),
        'norm_ffn': 1.0 + nrm(ks[8], (DEPTH, D_MODEL), 0.02),
        'w_gate': nrm(ks[9], (DEPTH, D_MODEL, D_FF), D_MODEL ** -0.5),
        'w_up': nrm(ks[10], (DEPTH, D_MODEL, D_FF), D_MODEL ** -0.5),
        'w_down': nrm(ks[11], (DEPTH, D_FF, D_MODEL), D_FF ** -0.5),
    }


def reference(x_prompt, x_sample, norm_mix, w_in, qk_norm, sink_a, rpb_c, w_out, norm_ffn, w_gate, w_up, w_down):
    y_prompt = trunk(x_prompt, norm_mix, w_in, qk_norm, sink_a, rpb_c, w_out, norm_ffn, w_gate, w_up, w_down)
    y_sample = trunk(x_sample, norm_mix, w_in, qk_norm, sink_a, rpb_c, w_out, norm_ffn, w_gate, w_up, w_down)
    return (y_prompt, y_sample)
```

```python
import functools

import jax
import jax.numpy as jnp
import numpy as np
from jax import lax
from jax.experimental import pallas as pl
from jax.experimental.pallas import tpu as pltpu

D_MODEL = 2048
SEQ = 8192
N_SEQ = 3
TOKENS = N_SEQ * SEQ
DEPTH = 2
HEAD_DIM = 128
A_Q_HEADS = 6
A_KV_HEADS = 2
A_GROUP = A_Q_HEADS // A_KV_HEADS
A_WINDOW = 128
B_DILATIONS = (1, 4, 16)
B_HALF = 64
B_HEADS = 6
C_HEADS = 4
GRID_W = 64
GRID_ROWS = SEQ // GRID_W
NA_ROWS = 8
NA_COLS = 16
MIX_WIDTH = (A_Q_HEADS + B_HEADS + C_HEADS) * HEAD_DIM
IN_WIDTH = (A_Q_HEADS + 2 * A_KV_HEADS + 3 * B_HEADS + 3 * C_HEADS) * HEAD_DIM
D_FF = 5632
ROT_DIM = HEAD_DIM // 4
ROPE_THETA = 500000.0
EPS = 1e-6
NEG = -1e30
SCALE = HEAD_DIM ** -0.5

BF16 = jnp.bfloat16
F32 = jnp.float32

VMEM_LIMIT_BYTES = 56 * 1024 * 1024

NAT_SLOTS = 28
NAT_WIDTH = NAT_SLOTS * HEAD_DIM
_CHUNKS = (
    (0, "rope", "nat", 0, (0, 0)), (2, "rope", "nat", 2, (0, 0)), (4, "rope", "nat", 4, (0, 0)),
    (6, "rope", "nat", 6, (0, 1)),
    (8, "plain", "nat", 8, None),
    (10, "rope", "nat", 10, (1, 0)), (16, "rope", "nat", 12, (1, 1)), (22, "plain", "nat", 14, None),
    (28, "norm", "nat", 16, (2, 0)), (30, "norm", "nat", 18, (2, 0)),
    (32, "norm", "nat", 20, (2, 1)), (34, "norm", "nat", 22, (2, 1)),
    (36, "plain", "nat", 24, None), (38, "plain", "nat", 26, None),
    (12, "rope", "d4", 0, (1, 0)), (18, "rope", "d4", 2, (1, 1)), (24, "plain", "d4", 4, None),
    (14, "rope", "d16", 0, (1, 0)), (20, "rope", "d16", 2, (1, 1)), (26, "plain", "d16", 4, None),
)
CHUNK_W = 2 * HEAD_DIM
NAT_QA_384 = 0
NAT_KA_128 = 6
NAT_VA_128 = 8
NAT_QB_256 = 5
NAT_KB_256 = 6
NAT_VB_256 = 7
NAT_QC_512 = 4
NAT_KC_512 = 5
NAT_VC_512 = 6

TM_PROJ = 512
TM_FFN = 512
TF_FFN = 512
TQ_A = 1024
TB_B = 2048
QB_B = 128
ROWS_C = 8
TOK_C = ROWS_C * GRID_W


def _rms_scale(y):
    return lax.rsqrt(jnp.mean(y * y, axis=-1, keepdims=True) + EPS)


def _inproj_kernel(x_ref, g_ref, w_ref, gain_ref, c_ref, sa_ref, sb_ref,
                   nat_ref, d4_ref, d16_ref, hn_ref, ybuf_ref):
    x = x_ref[...]
    hn_ref[...] = (x * _rms_scale(x) * g_ref[...]).astype(BF16)
    tm = x_ref.shape[0]
    for ci, (_, kind, dest, slot, _) in enumerate(_CHUNKS):
        acc = jnp.dot(hn_ref[...], w_ref[:, ci * CHUNK_W:(ci + 1) * CHUNK_W],
                      preferred_element_type=F32)
        for h in range(2):
            y = acc[:, h * HEAD_DIM:(h + 1) * HEAD_DIM]
            if kind != "plain":
                col = ci * CHUNK_W + h * HEAD_DIM
                y = y * _rms_scale(y) * gain_ref[:, col:col + HEAD_DIM]
            if kind == "rope":
                y = (y * c_ref[...] + pltpu.roll(y, HEAD_DIM - ROT_DIM // 2, 1) * sa_ref[...]
                     + pltpu.roll(y, ROT_DIM // 2, 1) * sb_ref[...])
            lo = (slot + h) * HEAD_DIM
            if dest == "nat":
                nat_ref[:, lo:lo + HEAD_DIM] = y.astype(BF16)
            else:
                d, out_ref = (4, d4_ref) if dest == "d4" else (16, d16_ref)
                ybuf_ref[...] = y
                for r in range(d):
                    out_ref[r, :, lo:lo + HEAD_DIM] = ybuf_ref[pl.ds(r, tm // d, stride=d), :].astype(BF16)


def _inproj(x, g, w, gain, rope_c, rope_sa, rope_sb):
    tm = TM_PROJ
    nt = SEQ // tm
    const = lambda i: (0, 0)
    rope_spec = pl.BlockSpec((tm, HEAD_DIM), lambda i: (i % nt, 0))
    return pl.pallas_call(
        _inproj_kernel,
        grid=(TOKENS // tm,),
        in_specs=[
            pl.BlockSpec((tm, D_MODEL), lambda i: (i, 0)),
            pl.BlockSpec((1, D_MODEL), const),
            pl.BlockSpec((D_MODEL, IN_WIDTH), const, pipeline_mode=pl.Buffered(1)),
            pl.BlockSpec((1, IN_WIDTH), const),
            rope_spec, rope_spec, rope_spec,
        ],
        out_specs=[
            pl.BlockSpec((tm, NAT_WIDTH), lambda i: (i, 0)),
            pl.BlockSpec((None, 4, tm // 4, 3 * CHUNK_W), lambda i: (i // nt, 0, i % nt, 0)),
            pl.BlockSpec((None, 16, tm // 16, 3 * CHUNK_W), lambda i: (i // nt, 0, i % nt, 0)),
        ],
        out_shape=[
            jax.ShapeDtypeStruct((TOKENS, NAT_WIDTH), BF16),
            jax.ShapeDtypeStruct((N_SEQ, 4, SEQ // 4, 3 * CHUNK_W), BF16),
            jax.ShapeDtypeStruct((N_SEQ, 16, SEQ // 16, 3 * CHUNK_W), BF16),
        ],
        scratch_shapes=[pltpu.VMEM((tm, D_MODEL), BF16), pltpu.VMEM((tm, HEAD_DIM), F32)],
        compiler_params=pltpu.CompilerParams(
            dimension_semantics=("parallel",), vmem_limit_bytes=VMEM_LIMIT_BYTES),
        name="inproj",
    )(x, g, w, gain, rope_c, rope_sa, rope_sb)


def _mixer_a_kernel(sink_ref, q_ref, kp_ref, km_ref, kn_ref, vp_ref, vm_ref, vn_ref,
                    o_ref, kcat_ref, vcat_ref):
    kv = pl.program_id(1)
    t = pl.program_id(2)
    tq = q_ref.shape[0]
    blk = A_WINDOW
    kcat_ref[0:blk, :] = kp_ref[...]
    kcat_ref[blk:blk + tq, :] = km_ref[...]
    kcat_ref[blk + tq:, :] = kn_ref[...]
    vcat_ref[0:blk, :] = vp_ref[...]
    vcat_ref[blk:blk + tq, :] = vm_ref[...]
    vcat_ref[blk + tq:, :] = vn_ref[...]

    rows = A_GROUP * blk
    row = lax.broadcasted_iota(jnp.int32, (rows, 3 * blk), 0)
    col = lax.broadcasted_iota(jnp.int32, (rows, 3 * blk), 1)
    rel = (col - blk) - (row % blk)
    band = (rel <= A_WINDOW) & (rel >= -A_WINDOW)
    sink = jnp.concatenate(
        [jnp.full((blk, 1), sink_ref[kv * A_GROUP + g], F32) for g in range(A_GROUP)], axis=0)

    def body(sub, carry):
        q0 = pl.multiple_of(sub * blk, blk)
        q3 = q_ref[pl.ds(q0, blk), :]
        qs = jnp.concatenate([q3[:, g * HEAD_DIM:(g + 1) * HEAD_DIM] for g in range(A_GROUP)], axis=0)
        kw = kcat_ref[pl.ds(q0, 3 * blk), :]
        vw = vcat_ref[pl.ds(q0, 3 * blk), :]
        s = lax.dot_general(qs, kw, (((1,), (1,)), ((), ())), preferred_element_type=F32) * SCALE
        kbase = t * tq + q0 - blk
        valid = band & (col >= -kbase) & (col < SEQ - kbase)
        s = jnp.where(valid, s, NEG)
        m = jnp.maximum(jnp.max(s, axis=-1, keepdims=True), sink)
        p = jnp.exp(s - m)
        denom = jnp.sum(p, axis=-1, keepdims=True) + jnp.exp(sink - m)
        o = jnp.dot(p.astype(BF16), vw, preferred_element_type=F32) * (1.0 / denom)
        for g in range(A_GROUP):
            o_ref[pl.ds(q0, blk), g * HEAD_DIM:(g + 1) * HEAD_DIM] = o[g * blk:(g + 1) * blk].astype(BF16)
        return carry

    lax.fori_loop(0, tq // blk, body, 0)


def _mixer_a(sink, nat):
    tq = TQ_A
    nt = SEQ // tq
    nb = tq // A_WINDOW
    last = TOKENS // A_WINDOW - 1

    def main(c0):
        return lambda b, kv, t: (b * nt + t, c0 + kv)

    def prev(c0):
        return lambda b, kv, t: (jnp.maximum((b * nt + t) * nb - 1, 0), c0 + kv)

    def nxt(c0):
        return lambda b, kv, t: (jnp.minimum((b * nt + t + 1) * nb, last), c0 + kv)

    halo = (A_WINDOW, HEAD_DIM)
    return pl.pallas_call(
        _mixer_a_kernel,
        grid=(N_SEQ, A_KV_HEADS, nt),
        in_specs=[
            pl.BlockSpec(memory_space=pltpu.SMEM),
            pl.BlockSpec((tq, A_GROUP * HEAD_DIM), main(NAT_QA_384)),
            pl.BlockSpec(halo, prev(NAT_KA_128)),
            pl.BlockSpec((tq, HEAD_DIM), main(NAT_KA_128)),
            pl.BlockSpec(halo, nxt(NAT_KA_128)),
            pl.BlockSpec(halo, prev(NAT_VA_128)),
            pl.BlockSpec((tq, HEAD_DIM), main(NAT_VA_128)),
            pl.BlockSpec(halo, nxt(NAT_VA_128)),
        ],
        out_specs=pl.BlockSpec((tq, A_GROUP * HEAD_DIM), main(0)),
        out_shape=jax.ShapeDtypeStruct((TOKENS, A_Q_HEADS * HEAD_DIM), BF16),
        scratch_shapes=[pltpu.VMEM((tq + 2 * A_WINDOW, HEAD_DIM), BF16),
                        pltpu.VMEM((tq + 2 * A_WINDOW, HEAD_DIM), BF16)],
        compiler_params=pltpu.CompilerParams(
            dimension_semantics=("parallel", "parallel", "parallel"), vmem_limit_bytes=VMEM_LIMIT_BYTES),
        name="mixer_a",
    )(sink, nat, nat, nat, nat, nat, nat, nat)


def _mixer_b_kernel(q0_ref, k0p_ref, k0m_ref, k0n_ref, v0p_ref, v0m_ref, v0n_ref,
                    q1_ref, k1p_ref, k1m_ref, k1n_ref, v1p_ref, v1m_ref, v1n_ref,
                    q2_ref, k2p_ref, k2m_ref, k2n_ref, v2p_ref, v2m_ref, v2n_ref,
                    o_ref,
                    kb0_ref, vb0_ref, kb1_ref, vb1_ref, kb2_ref, vb2_ref,
                    onat_ref, lnat_ref, ores_ref, lres_ref):
    t = pl.program_id(1)
    tb = o_ref.shape[0]
    groups = (
        (1, q0_ref, (k0p_ref, k0m_ref, k0n_ref), (v0p_ref, v0m_ref, v0n_ref), kb0_ref, vb0_ref),
        (4, q1_ref, (k1p_ref, k1m_ref, k1n_ref), (v1p_ref, v1m_ref, v1n_ref), kb1_ref, vb1_ref),
        (16, q2_ref, (k2p_ref, k2m_ref, k2n_ref), (v2p_ref, v2m_ref, v2n_ref), kb2_ref, vb2_ref),
    )
    row = lax.broadcasted_iota(jnp.int32, (QB_B, QB_B + 2 * B_HALF), 0)
    col = lax.broadcasted_iota(jnp.int32, (QB_B, QB_B + 2 * B_HALF), 1)
    rel = (col - B_HALF) - row
    band = (rel <= B_HALF) & (rel >= -B_HALF)

    for gi, (d, q_ref, k_refs, v_refs, kb_ref, vb_ref) in enumerate(groups):
        p_len = tb // d
        sub_len = SEQ // d
        nsub = p_len // QB_B
        for src, dst in ((k_refs, kb_ref), (v_refs, vb_ref)):
            if d == 1:
                dst[0, 0:B_HALF, :] = src[0][...]
                dst[0, B_HALF:B_HALF + p_len, :] = src[1][...]
                dst[0, B_HALF + p_len:, :] = src[2][...]
            else:
                dst[:, 0:B_HALF, :] = src[0][...]
                dst[:, B_HALF:B_HALF + p_len, :] = src[1][...]
                dst[:, B_HALF + p_len:, :] = src[2][...]

        def body(it, carry, d=d, gi=gi, q_ref=q_ref, kb_ref=kb_ref, vb_ref=vb_ref,
                 p_len=p_len, sub_len=sub_len, nsub=nsub):
            r = it // nsub
            p0 = pl.multiple_of((it % nsub) * QB_B, QB_B)
            kbase = t * p_len + p0 - B_HALF
            valid = band & (col >= -kbase) & (col < sub_len - kbase)
            for h in range(2):
                cs = slice(h * HEAD_DIM, (h + 1) * HEAD_DIM)
                if d == 1:
                    q = q_ref[pl.ds(p0, QB_B), cs]
                else:
                    q = q_ref[r, pl.ds(p0, QB_B), cs]
                kw = kb_ref[r, pl.ds(p0, QB_B + 2 * B_HALF), cs]
                vw = vb_ref[r, pl.ds(p0, QB_B + 2 * B_HALF), cs]
                s = lax.dot_general(q, kw, (((1,), (1,)), ((), ())), preferred_element_type=F32) * SCALE
                s = jnp.where(valid, s, NEG)
                m = jnp.max(s, axis=-1, keepdims=True)
                p = jnp.exp(s - m)
                denom = jnp.sum(p, axis=-1, keepdims=True)
                o = jnp.dot(p.astype(BF16), vw, preferred_element_type=F32) * (1.0 / denom)
                lse = jnp.broadcast_to(m + jnp.log(denom), (QB_B, HEAD_DIM))
                if d == 1:
                    onat_ref[gi, h, pl.ds(p0, QB_B), :] = o
                    lnat_ref[gi, h, pl.ds(p0, QB_B), :] = lse
                else:
                    res0 = pl.multiple_of(r * p_len + p0, QB_B)
                    ores_ref[h, pl.ds(res0, QB_B), :] = o
                    lres_ref[h, pl.ds(res0, QB_B), :] = lse
            return carry

        lax.fori_loop(0, d * nsub, body, 0)
        if d > 1:
            for r in range(d):
                for h in range(2):
                    rows_nat = pl.ds(r, p_len, stride=d)
                    onat_ref[gi, h, rows_nat, :] = ores_ref[h, r * p_len:(r + 1) * p_len, :]
                    lnat_ref[gi, h, rows_nat, :] = lres_ref[h, r * p_len:(r + 1) * p_len, :]

    chunk = 256

    def combine(ci, carry):
        r0 = pl.multiple_of(ci * chunk, chunk)
        rows = pl.ds(r0, chunk)
        for h in range(2):
            l0, l1, l2 = lnat_ref[0, h, rows, :], lnat_ref[1, h, rows, :], lnat_ref[2, h, rows, :]
            mx = jnp.maximum(jnp.maximum(l0, l1), l2)
            e0, e1, e2 = jnp.exp(l0 - mx), jnp.exp(l1 - mx), jnp.exp(l2 - mx)
            inv = 1.0 / (e0 + e1 + e2)
            for gi, e in enumerate((e0, e1, e2)):
                lo = gi * CHUNK_W + h * HEAD_DIM
                o_ref[rows, lo:lo + HEAD_DIM] = (onat_ref[gi, h, rows, :] * (e * inv)).astype(BF16)
        return carry

    lax.fori_loop(0, tb // chunk, combine, 0)


def _mixer_b(nat, d4, d16):
    tb = TB_B
    nt = SEQ // tb
    in_specs = []
    args = []
    nh1 = tb // B_HALF
    last1 = TOKENS // B_HALF - 1
    in_specs.append(pl.BlockSpec((tb, CHUNK_W), lambda b, t: (b * nt + t, NAT_QB_256)))
    args.append(nat)
    for c0 in (NAT_KB_256, NAT_VB_256):
        in_specs += [
            pl.BlockSpec((B_HALF, CHUNK_W), lambda b, t, c0=c0: (jnp.maximum((b * nt + t) * nh1 - 1, 0), c0)),
            pl.BlockSpec((tb, CHUNK_W), lambda b, t, c0=c0: (b * nt + t, c0)),
            pl.BlockSpec((B_HALF, CHUNK_W), lambda b, t, c0=c0: (jnp.minimum((b * nt + t + 1) * nh1, last1), c0)),
        ]
        args += [nat, nat, nat]
    for d, arr in ((4, d4), (16, d16)):
        p_len = tb // d
        nh = p_len // B_HALF
        last = SEQ // d // B_HALF - 1
        in_specs.append(pl.BlockSpec((None, d, p_len, CHUNK_W), lambda b, t: (b, 0, t, 0)))
        args.append(arr)
        for c0 in (1, 2):
            in_specs += [
                pl.BlockSpec((None, d, B_HALF, CHUNK_W),
                             lambda b, t, c0=c0, nh=nh: (b, 0, jnp.maximum(t * nh - 1, 0), c0)),
                pl.BlockSpec((None, d, p_len, CHUNK_W), lambda b, t, c0=c0: (b, 0, t, c0)),
                pl.BlockSpec((None, d, B_HALF, CHUNK_W),
                             lambda b, t, c0=c0, nh=nh, last=last: (b, 0, jnp.minimum((t + 1) * nh, last), c0)),
            ]
            args += [arr, arr, arr]
    scratch = []
    for d in B_DILATIONS:
        shape = (d, tb // d + 2 * B_HALF, CHUNK_W)
        scratch += [pltpu.VMEM(shape, BF16), pltpu.VMEM(shape, BF16)]
    scratch += [pltpu.VMEM((3, 2, tb, HEAD_DIM), F32), pltpu.VMEM((3, 2, tb, HEAD_DIM), F32),
                pltpu.VMEM((2, tb, HEAD_DIM), F32), pltpu.VMEM((2, tb, HEAD_DIM), F32)]
    return pl.pallas_call(
        _mixer_b_kernel,
        grid=(N_SEQ, nt),
        in_specs=in_specs,
        out_specs=pl.BlockSpec((tb, B_HEADS * HEAD_DIM), lambda b, t: (b * nt + t, 0)),
        out_shape=jax.ShapeDtypeStruct((TOKENS, B_HEADS * HEAD_DIM), BF16),
        scratch_shapes=scratch,
        compiler_params=pltpu.CompilerParams(
            dimension_semantics=("parallel", "parallel"), vmem_limit_bytes=VMEM_LIMIT_BYTES),
        name="mixer_b",
    )(*args)


def _mixer_c_kernel(bias_ref, q_ref, kp_ref, km_ref, kn_ref, vp_ref, vm_ref, vn_ref,
                    o_ref, kcat_ref, vcat_ref):
    t = pl.program_id(1)
    tok = q_ref.shape[0]
    kcat_ref[0:tok, :] = kp_ref[...]
    kcat_ref[tok:2 * tok, :] = km_ref[...]
    kcat_ref[2 * tok:, :] = kn_ref[...]
    vcat_ref[0:tok, :] = vp_ref[...]
    vcat_ref[tok:2 * tok, :] = vm_ref[...]
    vcat_ref[2 * tok:, :] = vn_ref[...]

    nk = NA_ROWS * GRID_W
    qcol = lax.broadcasted_iota(jnp.int32, (GRID_W, nk), 0)
    kcol = lax.broadcasted_iota(jnp.int32, (GRID_W, nk), 1) % GRID_W
    cstart = jnp.clip(qcol - NA_COLS // 2, 0, GRID_W - NA_COLS)
    col_valid = (kcol >= cstart) & (kcol < cstart + NA_COLS)

    def body(i, carry):
        r = t * ROWS_C + i
        r0 = jnp.clip(r - NA_ROWS // 2, 0, GRID_ROWS - NA_ROWS)
        off = pl.multiple_of((r0 - (t - 1) * ROWS_C) * GRID_W, GRID_W)
        rel0 = r0 - r + NA_ROWS - 1
        q0 = pl.multiple_of(i * GRID_W, GRID_W)
        for h in range(C_HEADS):
            cs = slice(h * HEAD_DIM, (h + 1) * HEAD_DIM)
            q = q_ref[pl.ds(q0, GRID_W), cs]
            kw = kcat_ref[pl.ds(off, nk), cs]
            vw = vcat_ref[pl.ds(off, nk), cs]
            s = lax.dot_general(q, kw, (((1,), (1,)), ((), ())), preferred_element_type=F32) * SCALE
            s = jnp.where(col_valid, s + bias_ref[h, rel0], NEG)
            m = jnp.max(s, axis=-1, keepdims=True)
            p = jnp.exp(s - m)
            denom = jnp.sum(p, axis=-1, keepdims=True)
            o = jnp.dot(p.astype(BF16), vw, preferred_element_type=F32) * (1.0 / denom)
            o_ref[pl.ds(q0, GRID_W), cs] = o.astype(BF16)
        return carry

    lax.fori_loop(0, ROWS_C, body, 0)


def _mixer_c(bias, nat):
    tok = TOK_C
    nt = SEQ // tok
    last = TOKENS // tok - 1
    width = C_HEADS * HEAD_DIM

    def main(c0):
        return lambda b, t: (b * nt + t, c0)

    def prev(c0):
        return lambda b, t: (jnp.maximum(b * nt + t - 1, 0), c0)

    def nxt(c0):
        return lambda b, t: (jnp.minimum(b * nt + t + 1, last), c0)

    blk = (tok, width)
    return pl.pallas_call(
        _mixer_c_kernel,
        grid=(N_SEQ, nt),
        in_specs=[
            pl.BlockSpec(bias.shape, lambda b, t: (0, 0, 0, 0)),
            pl.BlockSpec(blk, main(NAT_QC_512)),
            pl.BlockSpec(blk, prev(NAT_KC_512)), pl.BlockSpec(blk, main(NAT_KC_512)), pl.BlockSpec(blk, nxt(NAT_KC_512)),
            pl.BlockSpec(blk, prev(NAT_VC_512)), pl.BlockSpec(blk, main(NAT_VC_512)), pl.BlockSpec(blk, nxt(NAT_VC_512)),
        ],
        out_specs=pl.BlockSpec(blk, main(0)),
        out_shape=jax.ShapeDtypeStruct((TOKENS, width), BF16),
        scratch_shapes=[pltpu.VMEM((3 * tok, width), BF16), pltpu.VMEM((3 * tok, width), BF16)],
        compiler_params=pltpu.CompilerParams(
            dimension_semantics=("parallel", "parallel"), vmem_limit_bytes=VMEM_LIMIT_BYTES),
        name="mixer_c",
    )(bias, nat, nat, nat, nat, nat, nat, nat)


def _outproj_kernel(x_ref, oa_ref, ob_ref, oc_ref, w_ref, g_ref, x1_ref, hn_ref, mixed_ref):
    wa = oa_ref.shape[1]
    wb = ob_ref.shape[1]
    mixed_ref[:, 0:wa] = oa_ref[...]
    mixed_ref[:, wa:wa + wb] = ob_ref[...]
    mixed_ref[:, wa + wb:] = oc_ref[...]
    nc = 512
    ss = jnp.zeros((x_ref.shape[0], 1), F32)
    for c in range(D_MODEL // nc):
        cs = slice(c * nc, (c + 1) * nc)
        y = x_ref[:, cs] + jnp.dot(mixed_ref[...], w_ref[:, cs], preferred_element_type=F32)
        x1_ref[:, cs] = y
        ss = ss + jnp.sum(y * y, axis=-1, keepdims=True)
    inv = lax.rsqrt(ss * (1.0 / D_MODEL) + EPS)
    hn_ref[...] = (x1_ref[...] * inv * g_ref[...]).astype(BF16)


def _outproj(x, oa, ob, oc, w, g):
    tm = TM_PROJ
    const = lambda i: (0, 0)
    row = lambda i: (i, 0)
    return pl.pallas_call(
        _outproj_kernel,
        grid=(TOKENS // tm,),
        in_specs=[
            pl.BlockSpec((tm, D_MODEL), row),
            pl.BlockSpec((tm, oa.shape[1]), row),
            pl.BlockSpec((tm, ob.shape[1]), row),
            pl.BlockSpec((tm, oc.shape[1]), row),
            pl.BlockSpec((MIX_WIDTH, D_MODEL), const, pipeline_mode=pl.Buffered(1)),
            pl.BlockSpec((1, D_MODEL), const),
        ],
        out_specs=[pl.BlockSpec((tm, D_MODEL), row), pl.BlockSpec((tm, D_MODEL), row)],
        out_shape=[jax.ShapeDtypeStruct((TOKENS, D_MODEL), F32),
                   jax.ShapeDtypeStruct((TOKENS, D_MODEL), BF16)],
        scratch_shapes=[pltpu.VMEM((tm, MIX_WIDTH), BF16)],
        compiler_params=pltpu.CompilerParams(
            dimension_semantics=("parallel",), vmem_limit_bytes=VMEM_LIMIT_BYTES),
        name="outproj",
    )(x, oa, ob, oc, w, g)


def _ffn_kernel(hn_ref, x1_ref, wg_ref, wu_ref, wd_ref, o_ref):
    f = pl.program_id(1)
    hn = hn_ref[...]
    g = jnp.dot(hn, wg_ref[...], preferred_element_type=F32)
    u = jnp.dot(hn, wu_ref[...], preferred_element_type=F32)
    h = (g * jax.nn.sigmoid(g) * u).astype(BF16)
    y = jnp.dot(h, wd_ref[...], preferred_element_type=F32)

    @pl.when(f == 0)
    def _():
        o_ref[...] = x1_ref[...] + y

    @pl.when(f > 0)
    def _():
        o_ref[...] += y


def _ffn(hn, x1, wg, wu, wd):
    tm, tf = TM_FFN, TF_FFN
    row = lambda i, f: (i, 0)
    return pl.pallas_call(
        _ffn_kernel,
        grid=(TOKENS // tm, D_FF // tf),
        in_specs=[
            pl.BlockSpec((tm, D_MODEL), row),
            pl.BlockSpec((tm, D_MODEL), row),
            pl.BlockSpec((D_MODEL, tf), lambda i, f: (0, f)),
            pl.BlockSpec((D_MODEL, tf), lambda i, f: (0, f)),
            pl.BlockSpec((tf, D_MODEL), lambda i, f: (f, 0)),
        ],
        out_specs=pl.BlockSpec((tm, D_MODEL), row),
        out_shape=jax.ShapeDtypeStruct((TOKENS, D_MODEL), F32),
        compiler_params=pltpu.CompilerParams(
            dimension_semantics=("parallel", "arbitrary"), vmem_limit_bytes=VMEM_LIMIT_BYTES),
        name="ffn",
    )(hn, x1, wg, wu, wd)


def _rope_tables():
    half = ROT_DIM // 2
    inv = jnp.asarray((ROPE_THETA ** (-np.arange(0, ROT_DIM, 2, dtype=np.float32) / ROT_DIM)).astype(np.float32))
    ang = jnp.arange(SEQ, dtype=F32)[:, None] * inv[None, :]
    cos, sin = jnp.cos(ang), jnp.sin(ang)
    zeros = jnp.zeros((SEQ, HEAD_DIM - ROT_DIM), F32)
    zh = jnp.zeros((SEQ, half), F32)
    rope_c = jnp.concatenate([cos, cos, jnp.ones((SEQ, HEAD_DIM - ROT_DIM), F32)], axis=1)
    rope_sa = jnp.concatenate([-sin, zh, zeros], axis=1)
    rope_sb = jnp.concatenate([zh, sin, zeros], axis=1)
    return rope_c, rope_sa, rope_sb


def _inproj_params(w_in, qk_norm):
    cols = np.concatenate([np.arange(h0 * HEAD_DIM, (h0 + 2) * HEAD_DIM) for h0, *_ in _CHUNKS])
    w = w_in[:, cols].astype(BF16)
    gains = []
    for _, _, _, _, gidx in _CHUNKS:
        g = jnp.ones((HEAD_DIM,), F32) if gidx is None else qk_norm[gidx[0], gidx[1]].astype(F32)
        gains += [g, g]
    return w, jnp.concatenate(gains)[None, :]


def _neighbourhood_bias(rpb):
    rel_row = np.arange(NA_ROWS)[:, None] + np.arange(NA_ROWS)[None, :]
    rel_col = np.clip(np.arange(GRID_W)[None, :] - np.arange(GRID_W)[:, None] + NA_COLS - 1,
                      0, 2 * NA_COLS - 2)
    b = rpb.astype(F32)[:, rel_row[:, :, None, None], rel_col[None, None]]
    return jnp.transpose(b, (0, 1, 3, 2, 4)).reshape(C_HEADS, NA_ROWS, GRID_W, NA_ROWS * GRID_W)


def kernel(x_prompt, x_sample, norm_mix, w_in, qk_norm, sink_a, rpb_c, w_out, norm_ffn, w_gate, w_up, w_down):
    x = jnp.concatenate([x_prompt.reshape(-1, D_MODEL), x_sample.reshape(-1, D_MODEL)], axis=0)
    rope_c, rope_sa, rope_sb = _rope_tables()
    for l in range(DEPTH):
        w_in_l, gain_l = _inproj_params(w_in[l], qk_norm[l])
        nat, d4, d16 = _inproj(x, norm_mix[l][None, :], w_in_l, gain_l, rope_c, rope_sa, rope_sb)
        oa = _mixer_a(sink_a[l].astype(F32), nat)
        ob = _mixer_b(nat, d4, d16)
        oc = _mixer_c(_neighbourhood_bias(rpb_c[l]), nat)
        x1, hn = _outproj(x, oa, ob, oc, w_out[l].astype(BF16), norm_ffn[l][None, :])
        x = _ffn(hn, x1, w_gate[l].astype(BF16), w_up[l].astype(BF16), w_down[l].astype(BF16))
    y_prompt = x[:SEQ].reshape(x_prompt.shape)
    y_sample = x[SEQ:].reshape(x_sample.shape)
    return (y_prompt, y_sample)
```

```python
import functools

import jax
import jax.numpy as jnp
import numpy as np
from jax import lax
from jax.experimental import pallas as pl
from jax.experimental.pallas import tpu as pltpu

D_MODEL = 2048
SEQ = 8192
N_SEQ = 3
TOKENS = N_SEQ * SEQ
DEPTH = 2
HEAD_DIM = 128
A_Q_HEADS = 6
A_KV_HEADS = 2
A_GROUP = A_Q_HEADS // A_KV_HEADS
A_WINDOW = 128
B_DILATIONS = (1, 4, 16)
B_HALF = 64
B_HEADS = 6
C_HEADS = 4
GRID_W = 64
GRID_ROWS = SEQ // GRID_W
NA_ROWS = 8
NA_COLS = 16
MIX_WIDTH = (A_Q_HEADS + B_HEADS + C_HEADS) * HEAD_DIM
IN_WIDTH = (A_Q_HEADS + 2 * A_KV_HEADS + 3 * B_HEADS + 3 * C_HEADS) * HEAD_DIM
D_FF = 5632
ROT_DIM = HEAD_DIM // 4
ROPE_THETA = 500000.0
EPS = 1e-6
NEG = -1e30
SCALE = HEAD_DIM ** -0.5

BF16 = jnp.bfloat16
F32 = jnp.float32

VMEM_LIMIT_BYTES = 56 * 1024 * 1024

NAT_SLOTS = 28
NAT_WIDTH = NAT_SLOTS * HEAD_DIM
_CHUNKS = (
    (0, "rope", "nat", 0, (0, 0)), (2, "rope", "nat", 2, (0, 0)), (4, "rope", "nat", 4, (0, 0)),
    (6, "rope", "nat", 6, (0, 1)),
    (8, "plain", "nat", 8, None),
    (10, "rope", "nat", 10, (1, 0)), (16, "rope", "nat", 12, (1, 1)), (22, "plain", "nat", 14, None),
    (28, "norm", "nat", 16, (2, 0)), (30, "norm", "nat", 18, (2, 0)),
    (32, "norm", "nat", 20, (2, 1)), (34, "norm", "nat", 22, (2, 1)),
    (36, "plain", "nat", 24, None), (38, "plain", "nat", 26, None),
    (12, "rope", "d4", 0, (1, 0)), (18, "rope", "d4", 2, (1, 1)), (24, "plain", "d4", 4, None),
    (14, "rope", "d16", 0, (1, 0)), (20, "rope", "d16", 2, (1, 1)), (26, "plain", "d16", 4, None),
)
CHUNK_W = 2 * HEAD_DIM
NAT_QA_384 = 0
NAT_KA_128 = 6
NAT_VA_128 = 8
NAT_QB_256 = 5
NAT_KB_256 = 6
NAT_VB_256 = 7
NAT_QC_512 = 4
NAT_KC_512 = 5
NAT_VC_512 = 6

TM_PROJ = 512
TM_FFN = 512
TF_FFN = 512
TQ_A = 1024
TB_B = 2048
QB_B = 128
UNROLL_A = 2
UNROLL_B = 2
ROWS_C = 8
QROWS_C = 4
KROWS_C = QROWS_C + NA_ROWS
TOK_C = ROWS_C * GRID_W


def _rms_scale(y):
    return lax.rsqrt(jnp.mean(y * y, axis=-1, keepdims=True) + EPS)


def _inproj_kernel(x_ref, g_ref, w_ref, gain_ref, c_ref, sa_ref, sb_ref,
                   nat_ref, d4_ref, d16_ref, hn_ref, ybuf_ref):
    x = x_ref[...]
    hn_ref[...] = (x * _rms_scale(x) * g_ref[...]).astype(BF16)
    tm = x_ref.shape[0]
    for h0, kind, dest, slot, gidx in _CHUNKS:
        acc = jnp.dot(hn_ref[...], w_ref[:, h0 * HEAD_DIM:(h0 + 2) * HEAD_DIM],
                      preferred_element_type=F32)
        for h in range(2):
            y = acc[:, h * HEAD_DIM:(h + 1) * HEAD_DIM]
            if kind != "plain":
                gi = 2 * gidx[0] + gidx[1]
                y = y * _rms_scale(y) * gain_ref[gi:gi + 1, :]
            if kind == "rope":
                y = (y * c_ref[...] + pltpu.roll(y, HEAD_DIM - ROT_DIM // 2, 1) * sa_ref[...]
                     + pltpu.roll(y, ROT_DIM // 2, 1) * sb_ref[...])
            lo = (slot + h) * HEAD_DIM
            if dest == "nat":
                nat_ref[:, lo:lo + HEAD_DIM] = y.astype(BF16)
            else:
                d, out_ref = (4, d4_ref) if dest == "d4" else (16, d16_ref)
                ybuf_ref[...] = y
                for r in range(d):
                    out_ref[r, :, lo:lo + HEAD_DIM] = ybuf_ref[pl.ds(r, tm // d, stride=d), :].astype(BF16)


def _inproj(x, g, w, gain, rope_c, rope_sa, rope_sb):
    tm = TM_PROJ
    nt = SEQ // tm
    const = lambda i: (0, 0)
    rope_spec = pl.BlockSpec((tm, HEAD_DIM), lambda i: (i % nt, 0))
    return pl.pallas_call(
        _inproj_kernel,
        grid=(TOKENS // tm,),
        in_specs=[
            pl.BlockSpec((tm, D_MODEL), lambda i: (i, 0)),
            pl.BlockSpec((1, D_MODEL), const),
            pl.BlockSpec((D_MODEL, IN_WIDTH), const, pipeline_mode=pl.Buffered(1)),
            pl.BlockSpec((6, HEAD_DIM), const),
            rope_spec, rope_spec, rope_spec,
        ],
        out_specs=[
            pl.BlockSpec((tm, NAT_WIDTH), lambda i: (i, 0)),
            pl.BlockSpec((None, 4, tm // 4, 3 * CHUNK_W), lambda i: (i // nt, 0, i % nt, 0)),
            pl.BlockSpec((None, 16, tm // 16, 3 * CHUNK_W), lambda i: (i // nt, 0, i % nt, 0)),
        ],
        out_shape=[
            jax.ShapeDtypeStruct((TOKENS, NAT_WIDTH), BF16),
            jax.ShapeDtypeStruct((N_SEQ, 4, SEQ // 4, 3 * CHUNK_W), BF16),
            jax.ShapeDtypeStruct((N_SEQ, 16, SEQ // 16, 3 * CHUNK_W), BF16),
        ],
        scratch_shapes=[pltpu.VMEM((tm, D_MODEL), BF16), pltpu.VMEM((tm, HEAD_DIM), F32)],
        compiler_params=pltpu.CompilerParams(
            dimension_semantics=("parallel",), vmem_limit_bytes=VMEM_LIMIT_BYTES),
        name="inproj",
    )(x, g, w, gain, rope_c, rope_sa, rope_sb)


def _mixer_a_kernel(sink_ref, q_ref, kp_ref, km_ref, kn_ref, vp_ref, vm_ref, vn_ref,
                    o_ref, kcat_ref, vcat_ref):
    kv = pl.program_id(1)
    t = pl.program_id(2)
    tq = q_ref.shape[0]
    blk = A_WINDOW
    kcat_ref[0:blk, :] = kp_ref[...]
    kcat_ref[blk:blk + tq, :] = km_ref[...]
    kcat_ref[blk + tq:, :] = kn_ref[...]
    vcat_ref[0:blk, :] = vp_ref[...]
    vcat_ref[blk:blk + tq, :] = vm_ref[...]
    vcat_ref[blk + tq:, :] = vn_ref[...]

    rows = A_GROUP * blk
    row = lax.broadcasted_iota(jnp.int32, (rows, 3 * blk), 0)
    col = lax.broadcasted_iota(jnp.int32, (rows, 3 * blk), 1)
    rel = (col - blk) - (row % blk)
    band = (rel <= A_WINDOW) & (rel >= -A_WINDOW)
    sink = jnp.concatenate(
        [jnp.full((blk, 1), sink_ref[kv * A_GROUP + g], F32) for g in range(A_GROUP)], axis=0)

    def body(sub, carry):
        q0 = pl.multiple_of(sub * blk, blk)
        q3 = q_ref[pl.ds(q0, blk), :]
        qs = jnp.concatenate([q3[:, g * HEAD_DIM:(g + 1) * HEAD_DIM] for g in range(A_GROUP)], axis=0)
        kw = kcat_ref[pl.ds(q0, 3 * blk), :]
        vw = vcat_ref[pl.ds(q0, 3 * blk), :]
        s = lax.dot_general(qs, kw, (((1,), (1,)), ((), ())), preferred_element_type=F32) * SCALE
        kbase = t * tq + q0 - blk
        valid = band & (col >= -kbase) & (col < SEQ - kbase)
        s = jnp.where(valid, s, NEG)
        m = jnp.maximum(jnp.max(s, axis=-1, keepdims=True), sink)
        p = jnp.exp(s - m)
        denom = jnp.sum(p, axis=-1, keepdims=True) + jnp.exp(sink - m)
        o = jnp.dot(p.astype(BF16), vw, preferred_element_type=F32) * (1.0 / denom)
        for g in range(A_GROUP):
            o_ref[pl.ds(q0, blk), g * HEAD_DIM:(g + 1) * HEAD_DIM] = o[g * blk:(g + 1) * blk].astype(BF16)
        return carry

    lax.fori_loop(0, tq // blk, body, 0, unroll=UNROLL_A)


def _mixer_a(sink, nat):
    tq = TQ_A
    nt = SEQ // tq
    nb = tq // A_WINDOW
    last = TOKENS // A_WINDOW - 1

    def main(c0):
        return lambda b, kv, t: (b * nt + t, c0 + kv)

    def prev(c0):
        return lambda b, kv, t: (jnp.maximum((b * nt + t) * nb - 1, 0), c0 + kv)

    def nxt(c0):
        return lambda b, kv, t: (jnp.minimum((b * nt + t + 1) * nb, last), c0 + kv)

    halo = (A_WINDOW, HEAD_DIM)
    return pl.pallas_call(
        _mixer_a_kernel,
        grid=(N_SEQ, A_KV_HEADS, nt),
        in_specs=[
            pl.BlockSpec(memory_space=pltpu.SMEM),
            pl.BlockSpec((tq, A_GROUP * HEAD_DIM), main(NAT_QA_384)),
            pl.BlockSpec(halo, prev(NAT_KA_128)),
            pl.BlockSpec((tq, HEAD_DIM), main(NAT_KA_128)),
            pl.BlockSpec(halo, nxt(NAT_KA_128)),
            pl.BlockSpec(halo, prev(NAT_VA_128)),
            pl.BlockSpec((tq, HEAD_DIM), main(NAT_VA_128)),
            pl.BlockSpec(halo, nxt(NAT_VA_128)),
        ],
        out_specs=pl.BlockSpec((tq, A_GROUP * HEAD_DIM), main(0)),
        out_shape=jax.ShapeDtypeStruct((TOKENS, A_Q_HEADS * HEAD_DIM), BF16),
        scratch_shapes=[pltpu.VMEM((tq + 2 * A_WINDOW, HEAD_DIM), BF16),
                        pltpu.VMEM((tq + 2 * A_WINDOW, HEAD_DIM), BF16)],
        compiler_params=pltpu.CompilerParams(
            dimension_semantics=("parallel", "parallel", "parallel"), vmem_limit_bytes=VMEM_LIMIT_BYTES),
        name="mixer_a",
    )(sink, nat, nat, nat, nat, nat, nat, nat)


def _mixer_b_kernel(q0_ref, k0p_ref, k0m_ref, k0n_ref, v0p_ref, v0m_ref, v0n_ref,
                    q1_ref, k1p_ref, k1m_ref, k1n_ref, v1p_ref, v1m_ref, v1n_ref,
                    q2_ref, k2p_ref, k2m_ref, k2n_ref, v2p_ref, v2m_ref, v2n_ref,
                    o_ref,
                    kb0_ref, vb0_ref, kb1_ref, vb1_ref, kb2_ref, vb2_ref,
                    onat_ref, lnat_ref, ores_ref, lres_ref):
    t = pl.program_id(1)
    tb = o_ref.shape[0]
    groups = (
        (1, q0_ref, (k0p_ref, k0m_ref, k0n_ref), (v0p_ref, v0m_ref, v0n_ref), kb0_ref, vb0_ref),
        (4, q1_ref, (k1p_ref, k1m_ref, k1n_ref), (v1p_ref, v1m_ref, v1n_ref), kb1_ref, vb1_ref),
        (16, q2_ref, (k2p_ref, k2m_ref, k2n_ref), (v2p_ref, v2m_ref, v2n_ref), kb2_ref, vb2_ref),
    )
    row = lax.broadcasted_iota(jnp.int32, (QB_B, QB_B + 2 * B_HALF), 0)
    col = lax.broadcasted_iota(jnp.int32, (QB_B, QB_B + 2 * B_HALF), 1)
    rel = (col - B_HALF) - row
    band = (rel <= B_HALF) & (rel >= -B_HALF)

    for gi, (d, q_ref, k_refs, v_refs, kb_ref, vb_ref) in enumerate(groups):
        p_len = tb // d
        sub_len = SEQ // d
        nsub = p_len // QB_B
        for src, dst in ((k_refs, kb_ref), (v_refs, vb_ref)):
            if d == 1:
                dst[0, 0:B_HALF, :] = src[0][...]
                dst[0, B_HALF:B_HALF + p_len, :] = src[1][...]
                dst[0, B_HALF + p_len:, :] = src[2][...]
            else:
                dst[:, 0:B_HALF, :] = src[0][...]
                dst[:, B_HALF:B_HALF + p_len, :] = src[1][...]
                dst[:, B_HALF + p_len:, :] = src[2][...]

        def body(it, carry, d=d, gi=gi, q_ref=q_ref, kb_ref=kb_ref, vb_ref=vb_ref,
                 p_len=p_len, sub_len=sub_len, nsub=nsub):
            r = it // nsub
            p0 = pl.multiple_of((it % nsub) * QB_B, QB_B)
            kbase = t * p_len + p0 - B_HALF
            valid = band & (col >= -kbase) & (col < sub_len - kbase)
            for h in range(2):
                cs = slice(h * HEAD_DIM, (h + 1) * HEAD_DIM)
                if d == 1:
                    q = q_ref[pl.ds(p0, QB_B), cs]
                else:
                    q = q_ref[r, pl.ds(p0, QB_B), cs]
                kw = kb_ref[r, pl.ds(p0, QB_B + 2 * B_HALF), cs]
                vw = vb_ref[r, pl.ds(p0, QB_B + 2 * B_HALF), cs]
                s = lax.dot_general(q, kw, (((1,), (1,)), ((), ())), preferred_element_type=F32) * SCALE
                s = jnp.where(valid, s, NEG)
                m = jnp.max(s, axis=-1, keepdims=True)
                p = jnp.exp(s - m)
                denom = jnp.sum(p, axis=-1, keepdims=True)
                o = jnp.dot(p.astype(BF16), vw, preferred_element_type=F32) * (1.0 / denom)
                lse = jnp.broadcast_to(m + jnp.log(denom), (QB_B, HEAD_DIM))
                if d == 1:
                    onat_ref[gi, h, pl.ds(p0, QB_B), :] = o
                    lnat_ref[gi, h, pl.ds(p0, QB_B), :] = lse
                else:
                    res0 = pl.multiple_of(r * p_len + p0, QB_B)
                    ores_ref[h, pl.ds(res0, QB_B), :] = o
                    lres_ref[h, pl.ds(res0, QB_B), :] = lse
            return carry

        lax.fori_loop(0, d * nsub, body, 0, unroll=UNROLL_B)
        if d > 1:
            for r in range(d):
                for h in range(2):
                    rows_nat = pl.ds(r, p_len, stride=d)
                    onat_ref[gi, h, rows_nat, :] = ores_ref[h, r * p_len:(r + 1) * p_len, :]
                    lnat_ref[gi, h, rows_nat, :] = lres_ref[h, r * p_len:(r + 1) * p_len, :]

    chunk = 256

    def combine(ci, carry):
        r0 = pl.multiple_of(ci * chunk, chunk)
        rows = pl.ds(r0, chunk)
        for h in range(2):
            l0, l1, l2 = lnat_ref[0, h, rows, :], lnat_ref[1, h, rows, :], lnat_ref[2, h, rows, :]
            mx = jnp.maximum(jnp.maximum(l0, l1), l2)
            e0, e1, e2 = jnp.exp(l0 - mx), jnp.exp(l1 - mx), jnp.exp(l2 - mx)
            inv = 1.0 / (e0 + e1 + e2)
            for gi, e in enumerate((e0, e1, e2)):
                lo = gi * CHUNK_W + h * HEAD_DIM
                o_ref[rows, lo:lo + HEAD_DIM] = (onat_ref[gi, h, rows, :] * (e * inv)).astype(BF16)
        return carry

    lax.fori_loop(0, tb // chunk, combine, 0)


def _mixer_b(nat, d4, d16):
    tb = TB_B
    nt = SEQ // tb
    in_specs = []
    args = []
    nh1 = tb // B_HALF
    last1 = TOKENS // B_HALF - 1
    in_specs.append(pl.BlockSpec((tb, CHUNK_W), lambda b, t: (b * nt + t, NAT_QB_256)))
    args.append(nat)
    for c0 in (NAT_KB_256, NAT_VB_256):
        in_specs += [
            pl.BlockSpec((B_HALF, CHUNK_W), lambda b, t, c0=c0: (jnp.maximum((b * nt + t) * nh1 - 1, 0), c0)),
            pl.BlockSpec((tb, CHUNK_W), lambda b, t, c0=c0: (b * nt + t, c0)),
            pl.BlockSpec((B_HALF, CHUNK_W), lambda b, t, c0=c0: (jnp.minimum((b * nt + t + 1) * nh1, last1), c0)),
        ]
        args += [nat, nat, nat]
    for d, arr in ((4, d4), (16, d16)):
        p_len = tb // d
        nh = p_len // B_HALF
        last = SEQ // d // B_HALF - 1
        in_specs.append(pl.BlockSpec((None, d, p_len, CHUNK_W), lambda b, t: (b, 0, t, 0)))
        args.append(arr)
        for c0 in (1, 2):
            in_specs += [
                pl.BlockSpec((None, d, B_HALF, CHUNK_W),
                             lambda b, t, c0=c0, nh=nh: (b, 0, jnp.maximum(t * nh - 1, 0), c0)),
                pl.BlockSpec((None, d, p_len, CHUNK_W), lambda b, t, c0=c0: (b, 0, t, c0)),
                pl.BlockSpec((None, d, B_HALF, CHUNK_W),
                             lambda b, t, c0=c0, nh=nh, last=last: (b, 0, jnp.minimum((t + 1) * nh, last), c0)),
            ]
            args += [arr, arr, arr]
    scratch = []
    for d in B_DILATIONS:
        shape = (d, tb // d + 2 * B_HALF, CHUNK_W)
        scratch += [pltpu.VMEM(shape, BF16), pltpu.VMEM(shape, BF16)]
    scratch += [pltpu.VMEM((3, 2, tb, HEAD_DIM), F32), pltpu.VMEM((3, 2, tb, HEAD_DIM), F32),
                pltpu.VMEM((2, tb, HEAD_DIM), F32), pltpu.VMEM((2, tb, HEAD_DIM), F32)]
    return pl.pallas_call(
        _mixer_b_kernel,
        grid=(N_SEQ, nt),
        in_specs=in_specs,
        out_specs=pl.BlockSpec((tb, B_HEADS * HEAD_DIM), lambda b, t: (b * nt + t, 0)),
        out_shape=jax.ShapeDtypeStruct((TOKENS, B_HEADS * HEAD_DIM), BF16),
        scratch_shapes=scratch,
        compiler_params=pltpu.CompilerParams(
            dimension_semantics=("parallel", "parallel"), vmem_limit_bytes=VMEM_LIMIT_BYTES),
        name="mixer_b",
    )(*args)


def _mixer_c_kernel(bias_ref, q_ref, kp_ref, km_ref, kn_ref, vp_ref, vm_ref, vn_ref,
                    o_ref, kcat_ref, vcat_ref):
    t = pl.program_id(1)
    tok = q_ref.shape[0]
    kcat_ref[0:tok, :] = kp_ref[...]
    kcat_ref[tok:2 * tok, :] = km_ref[...]
    kcat_ref[2 * tok:, :] = kn_ref[...]
    vcat_ref[0:tok, :] = vp_ref[...]
    vcat_ref[tok:2 * tok, :] = vm_ref[...]
    vcat_ref[2 * tok:, :] = vn_ref[...]

    nq = QROWS_C * GRID_W
    nk = KROWS_C * GRID_W
    row = lax.broadcasted_iota(jnp.int32, (nq, nk), 0)
    col = lax.broadcasted_iota(jnp.int32, (nq, nk), 1)
    qi, qc = row // GRID_W, row % GRID_W
    kj, kc = col // GRID_W, col % GRID_W
    cstart = jnp.clip(qc - NA_COLS // 2, 0, GRID_W - NA_COLS)
    col_valid = (kc >= cstart) & (kc < cstart + NA_COLS)
    for j in range(ROWS_C // QROWS_C):
        r = t * ROWS_C + j * QROWS_C
        first = jnp.clip(r + qi - NA_ROWS // 2, 0, GRID_ROWS - NA_ROWS)
        krel = (r - NA_ROWS // 2) + kj - first
        valid = col_valid & (krel >= 0) & (krel < NA_ROWS)
        k0 = (ROWS_C + j * QROWS_C - NA_ROWS // 2) * GRID_W
        for h in range(C_HEADS):
            cs = slice(h * HEAD_DIM, (h + 1) * HEAD_DIM)
            q = q_ref[j * nq:(j + 1) * nq, cs]
            kw = kcat_ref[k0:k0 + nk, cs]
            vw = vcat_ref[k0:k0 + nk, cs]
            s = lax.dot_general(q, kw, (((1,), (1,)), ((), ())), preferred_element_type=F32) * SCALE
            s = jnp.where(valid, s + bias_ref[h], NEG)
            m = jnp.max(s, axis=-1, keepdims=True)
            p = jnp.exp(s - m)
            denom = jnp.sum(p, axis=-1, keepdims=True)
            o = jnp.dot(p.astype(BF16), vw, preferred_element_type=F32) * (1.0 / denom)
            o_ref[j * nq:(j + 1) * nq, cs] = o.astype(BF16)


def _mixer_c(bias, nat):
    tok = TOK_C
    nt = SEQ // tok
    last = TOKENS // tok - 1
    width = C_HEADS * HEAD_DIM

    def main(c0):
        return lambda b, t: (b * nt + t, c0)

    def prev(c0):
        return lambda b, t: (jnp.maximum(b * nt + t - 1, 0), c0)

    def nxt(c0):
        return lambda b, t: (jnp.minimum(b * nt + t + 1, last), c0)

    blk = (tok, width)
    return pl.pallas_call(
        _mixer_c_kernel,
        grid=(N_SEQ, nt),
        in_specs=[
            pl.BlockSpec(bias.shape, lambda b, t: (0, 0, 0)),
            pl.BlockSpec(blk, main(NAT_QC_512)),
            pl.BlockSpec(blk, prev(NAT_KC_512)), pl.BlockSpec(blk, main(NAT_KC_512)), pl.BlockSpec(blk, nxt(NAT_KC_512)),
            pl.BlockSpec(blk, prev(NAT_VC_512)), pl.BlockSpec(blk, main(NAT_VC_512)), pl.BlockSpec(blk, nxt(NAT_VC_512)),
        ],
        out_specs=pl.BlockSpec(blk, main(0)),
        out_shape=jax.ShapeDtypeStruct((TOKENS, width), BF16),
        scratch_shapes=[pltpu.VMEM((3 * tok, width), BF16), pltpu.VMEM((3 * tok, width), BF16)],
        compiler_params=pltpu.CompilerParams(
            dimension_semantics=("parallel", "parallel"), vmem_limit_bytes=VMEM_LIMIT_BYTES),
        name="mixer_c",
    )(bias, nat, nat, nat, nat, nat, nat, nat)


def _outproj_kernel(x_ref, oa_ref, ob_ref, oc_ref, w_ref, g_ref, x1_ref, hn_ref, mixed_ref):
    wa = oa_ref.shape[1]
    wb = ob_ref.shape[1]
    mixed_ref[:, 0:wa] = oa_ref[...]
    mixed_ref[:, wa:wa + wb] = ob_ref[...]
    mixed_ref[:, wa + wb:] = oc_ref[...]
    nc = 512
    ss = jnp.zeros((x_ref.shape[0], 1), F32)
    for c in range(D_MODEL // nc):
        cs = slice(c * nc, (c + 1) * nc)
        y = x_ref[:, cs] + jnp.dot(mixed_ref[...], w_ref[:, cs], preferred_element_type=F32)
        x1_ref[:, cs] = y
        ss = ss + jnp.sum(y * y, axis=-1, keepdims=True)
    inv = lax.rsqrt(ss * (1.0 / D_MODEL) + EPS)
    hn_ref[...] = (x1_ref[...] * inv * g_ref[...]).astype(BF16)


def _outproj(x, oa, ob, oc, w, g):
    tm = TM_PROJ
    const = lambda i: (0, 0)
    row = lambda i: (i, 0)
    return pl.pallas_call(
        _outproj_kernel,
        grid=(TOKENS // tm,),
        in_specs=[
            pl.BlockSpec((tm, D_MODEL), row),
            pl.BlockSpec((tm, oa.shape[1]), row),
            pl.BlockSpec((tm, ob.shape[1]), row),
            pl.BlockSpec((tm, oc.shape[1]), row),
            pl.BlockSpec((MIX_WIDTH, D_MODEL), const, pipeline_mode=pl.Buffered(1)),
            pl.BlockSpec((1, D_MODEL), const),
        ],
        out_specs=[pl.BlockSpec((tm, D_MODEL), row), pl.BlockSpec((tm, D_MODEL), row)],
        out_shape=[jax.ShapeDtypeStruct((TOKENS, D_MODEL), F32),
                   jax.ShapeDtypeStruct((TOKENS, D_MODEL), BF16)],
        scratch_shapes=[pltpu.VMEM((tm, MIX_WIDTH), BF16)],
        compiler_params=pltpu.CompilerParams(
            dimension_semantics=("parallel",), vmem_limit_bytes=VMEM_LIMIT_BYTES),
        name="outproj",
    )(x, oa, ob, oc, w, g)


def _ffn_kernel(hn_ref, x1_ref, wg_ref, wu_ref, wd_ref, o_ref):
    @pl.when(pl.program_id(1) == 0)
    def _():
        o_ref[...] = x1_ref[...]

    hn = hn_ref[...]
    g = jnp.dot(hn, wg_ref[...], preferred_element_type=F32)
    u = jnp.dot(hn, wu_ref[...], preferred_element_type=F32)
    h = (g * jax.nn.sigmoid(g) * u).astype(BF16)
    nc = 512
    for c in range(D_MODEL // nc):
        cs = slice(c * nc, (c + 1) * nc)
        o_ref[:, cs] += jnp.dot(h, wd_ref[:, cs], preferred_element_type=F32)


def _ffn(hn, x1, wg, wu, wd):
    tm, tf = TM_FFN, TF_FFN
    row = lambda i, f: (i, 0)
    return pl.pallas_call(
        _ffn_kernel,
        grid=(TOKENS // tm, D_FF // tf),
        in_specs=[
            pl.BlockSpec((tm, D_MODEL), row),
            pl.BlockSpec((tm, D_MODEL), row),
            pl.BlockSpec((D_MODEL, tf), lambda i, f: (0, f)),
            pl.BlockSpec((D_MODEL, tf), lambda i, f: (0, f)),
            pl.BlockSpec((tf, D_MODEL), lambda i, f: (f, 0)),
        ],
        out_specs=pl.BlockSpec((tm, D_MODEL), row),
        out_shape=jax.ShapeDtypeStruct((TOKENS, D_MODEL), F32),
        compiler_params=pltpu.CompilerParams(
            dimension_semantics=("parallel", "arbitrary"), vmem_limit_bytes=VMEM_LIMIT_BYTES),
        name="ffn",
    )(hn, x1, wg, wu, wd)


def _rope_tables():
    half = ROT_DIM // 2
    inv = jnp.asarray((ROPE_THETA ** (-np.arange(0, ROT_DIM, 2, dtype=np.float32) / ROT_DIM)).astype(np.float32))
    ang = jnp.arange(SEQ, dtype=F32)[:, None] * inv[None, :]
    cos, sin = jnp.cos(ang), jnp.sin(ang)
    zeros = jnp.zeros((SEQ, HEAD_DIM - ROT_DIM), F32)
    zh = jnp.zeros((SEQ, half), F32)
    rope_c = jnp.concatenate([cos, cos, jnp.ones((SEQ, HEAD_DIM - ROT_DIM), F32)], axis=1)
    rope_sa = jnp.concatenate([-sin, zh, zeros], axis=1)
    rope_sb = jnp.concatenate([zh, sin, zeros], axis=1)
    return rope_c, rope_sa, rope_sb


def _neighbourhood_bias(rpb):
    rel_row = np.arange(KROWS_C)[None, :] - np.arange(QROWS_C)[:, None] + NA_ROWS // 2 - 1
    rel_col = np.clip(np.arange(GRID_W)[None, :] - np.arange(GRID_W)[:, None] + NA_COLS - 1,
                      0, 2 * NA_COLS - 2)
    b = rpb.astype(F32)[:, rel_row[:, None, :, None], rel_col[None, :, None, :]]
    return b.reshape(C_HEADS, QROWS_C * GRID_W, KROWS_C * GRID_W)


def kernel(x_prompt, x_sample, norm_mix, w_in, qk_norm, sink_a, rpb_c, w_out, norm_ffn, w_gate, w_up, w_down):
    x = jnp.concatenate([x_prompt.reshape(-1, D_MODEL), x_sample.reshape(-1, D_MODEL)], axis=0)
    rope_c, rope_sa, rope_sb = _rope_tables()
    for l in range(DEPTH):
        nat, d4, d16 = _inproj(x, norm_mix[l][None, :], w_in[l].astype(BF16),
                               qk_norm[l].reshape(6, HEAD_DIM).astype(F32), rope_c, rope_sa, rope_sb)
        oa = _mixer_a(sink_a[l].astype(F32), nat)
        ob = _mixer_b(nat, d4, d16)
        oc = _mixer_c(_neighbourhood_bias(rpb_c[l]), nat)
        x1, hn = _outproj(x, oa, ob, oc, w_out[l].astype(BF16), norm_ffn[l][None, :])
        x = _ffn(hn, x1, w_gate[l].astype(BF16), w_up[l].astype(BF16), w_down[l].astype(BF16))
    y_prompt = x[:SEQ].reshape(x_prompt.shape)
    y_sample = x[SEQ:].reshape(x_sample.shape)
    return (y_prompt, y_sample)
```

```python
import functools

import jax
import jax.numpy as jnp
import numpy as np
from jax import lax
from jax.experimental import pallas as pl
from jax.experimental.pallas import tpu as pltpu

D_MODEL = 2048
SEQ = 8192
N_SEQ = 3
TOKENS = N_SEQ * SEQ
DEPTH = 2
HEAD_DIM = 128
A_Q_HEADS = 6
A_KV_HEADS = 2
A_GROUP = A_Q_HEADS // A_KV_HEADS
A_WINDOW = 128
B_DILATIONS = (1, 4, 16)
B_HALF = 64
B_HEADS = 6
C_HEADS = 4
GRID_W = 64
GRID_ROWS = SEQ // GRID_W
NA_ROWS = 8
NA_COLS = 16
MIX_WIDTH = (A_Q_HEADS + B_HEADS + C_HEADS) * HEAD_DIM
IN_WIDTH = (A_Q_HEADS + 2 * A_KV_HEADS + 3 * B_HEADS + 3 * C_HEADS) * HEAD_DIM
D_FF = 5632
ROT_DIM = HEAD_DIM // 4
ROPE_THETA = 500000.0
EPS = 1e-6
NEG = -1e30
SCALE = HEAD_DIM ** -0.5

BF16 = jnp.bfloat16
F32 = jnp.float32

VMEM_LIMIT_BYTES = 56 * 1024 * 1024

NAT_SLOTS = 28
NAT_WIDTH = NAT_SLOTS * HEAD_DIM
_CHUNKS = (
    (0, "rope", "nat", 0, (0, 0)), (2, "rope", "nat", 2, (0, 0)), (4, "rope", "nat", 4, (0, 0)),
    (6, "rope", "nat", 6, (0, 1)),
    (8, "plain", "nat", 8, None),
    (10, "rope", "nat", 10, (1, 0)), (16, "rope", "nat", 12, (1, 1)), (22, "plain", "nat", 14, None),
    (28, "norm", "nat", 16, (2, 0)), (30, "norm", "nat", 18, (2, 0)),
    (32, "norm", "nat", 20, (2, 1)), (34, "norm", "nat", 22, (2, 1)),
    (36, "plain", "nat", 24, None), (38, "plain", "nat", 26, None),
    (12, "rope", "d4", 0, (1, 0)), (18, "rope", "d4", 2, (1, 1)), (24, "plain", "d4", 4, None),
    (14, "rope", "d16", 0, (1, 0)), (20, "rope", "d16", 2, (1, 1)), (26, "plain", "d16", 4, None),
)
CHUNK_W = 2 * HEAD_DIM
NAT_QA_384 = 0
NAT_KA_128 = 6
NAT_VA_128 = 8
NAT_QB_256 = 5
NAT_KB_256 = 6
NAT_VB_256 = 7
NAT_QC_512 = 4
NAT_KC_512 = 5
NAT_VC_512 = 6

TM_PROJ = 512
TM_FFN = 512
TF_FFN = 512
TQ_A = 1024
TB_B = 2048
QB_B = 128
UNROLL_A = 2
UNROLL_B = 2
ROWS_C = 8
QROWS_C = 4
KROWS_C = QROWS_C + NA_ROWS
TOK_C = ROWS_C * GRID_W


def _rms_scale(y):
    return lax.rsqrt(jnp.mean(y * y, axis=-1, keepdims=True) + EPS)


def _row_tiles(arrays, tm):
    bounds, lo = [], 0
    for a in arrays:
        bounds.append((lo, lo + a.shape[0] // tm))
        lo = bounds[-1][1]
    return tuple(bounds)


def _stacked_spec(bounds_k, tm, width, grid_rank):
    lo, hi = bounds_k
    if grid_rank == 1:
        return pl.BlockSpec((tm, width), lambda i: (jnp.clip(i - lo, 0, hi - lo - 1), 0))
    return pl.BlockSpec((tm, width), lambda i, f: (jnp.clip(i - lo, 0, hi - lo - 1), 0))


def _for_owner(bounds, refs, fn):
    if len(refs) == 1:
        fn(refs[0])
        return
    i = pl.program_id(0)
    for (lo, hi), ref in zip(bounds, refs):
        pl.when((i >= lo) & (i < hi))(functools.partial(fn, ref))


def _inproj_kernel(bounds, *refs):
    nx = len(bounds)
    x_refs = refs[:nx]
    g_ref, w_ref, gain_ref, rope_ref, nat_ref, d4_ref, d16_ref, hn_ref, ybuf_ref = refs[nx:]

    def prenorm(x_ref):
        x = x_ref[...]
        hn_ref[...] = (x * _rms_scale(x) * g_ref[...]).astype(BF16)

    _for_owner(bounds, x_refs, prenorm)
    tm = hn_ref.shape[0]
    half = ROT_DIM // 2
    table = rope_ref[...]
    lane = lax.broadcasted_iota(jnp.int32, table.shape, 1)
    sin_pair = pltpu.roll(table, HEAD_DIM - ROT_DIM, 1)
    rope_c = jnp.where(lane < ROT_DIM, table, 1.0)
    rope_sa = jnp.where(lane < half, sin_pair, 0.0)
    rope_sb = jnp.where((lane >= half) & (lane < ROT_DIM), sin_pair, 0.0)
    for h0, kind, dest, slot, gidx in _CHUNKS:
        acc = jnp.dot(hn_ref[...], w_ref[:, h0 * HEAD_DIM:(h0 + 2) * HEAD_DIM],
                      preferred_element_type=F32)
        for h in range(2):
            y = acc[:, h * HEAD_DIM:(h + 1) * HEAD_DIM]
            if kind != "plain":
                gi = 2 * gidx[0] + gidx[1]
                y = y * _rms_scale(y) * gain_ref[gi:gi + 1, :]
            if kind == "rope":
                y = (y * rope_c + pltpu.roll(y, HEAD_DIM - half, 1) * rope_sa
                     + pltpu.roll(y, half, 1) * rope_sb)
            lo = (slot + h) * HEAD_DIM
            if dest == "nat":
                nat_ref[:, lo:lo + HEAD_DIM] = y.astype(BF16)
            else:
                d, out_ref = (4, d4_ref) if dest == "d4" else (16, d16_ref)
                ybuf_ref[...] = y
                for r in range(d):
                    out_ref[r, :, lo:lo + HEAD_DIM] = ybuf_ref[pl.ds(r, tm // d, stride=d), :].astype(BF16)


def _inproj(xs, g, w, gain, rope):
    tm = TM_PROJ
    nt = SEQ // tm
    const = lambda i: (0, 0)
    bounds = _row_tiles(xs, tm)
    return pl.pallas_call(
        functools.partial(_inproj_kernel, bounds),
        grid=(TOKENS // tm,),
        in_specs=[_stacked_spec(b, tm, D_MODEL, 1) for b in bounds] + [
            pl.BlockSpec((1, D_MODEL), const),
            pl.BlockSpec((D_MODEL, IN_WIDTH), const, pipeline_mode=pl.Buffered(1)),
            pl.BlockSpec((6, HEAD_DIM), const),
            pl.BlockSpec((tm, HEAD_DIM), lambda i: (i % nt, 0)),
        ],
        out_specs=[
            pl.BlockSpec((tm, NAT_WIDTH), lambda i: (i, 0)),
            pl.BlockSpec((None, 4, tm // 4, 3 * CHUNK_W), lambda i: (i // nt, 0, i % nt, 0)),
            pl.BlockSpec((None, 16, tm // 16, 3 * CHUNK_W), lambda i: (i // nt, 0, i % nt, 0)),
        ],
        out_shape=[
            jax.ShapeDtypeStruct((TOKENS, NAT_WIDTH), BF16),
            jax.ShapeDtypeStruct((N_SEQ, 4, SEQ // 4, 3 * CHUNK_W), BF16),
            jax.ShapeDtypeStruct((N_SEQ, 16, SEQ // 16, 3 * CHUNK_W), BF16),
        ],
        scratch_shapes=[pltpu.VMEM((tm, D_MODEL), BF16), pltpu.VMEM((tm, HEAD_DIM), F32)],
        compiler_params=pltpu.CompilerParams(
            dimension_semantics=("parallel",), vmem_limit_bytes=VMEM_LIMIT_BYTES),
        name="inproj",
    )(*xs, g, w, gain, rope)


def _mixer_a_kernel(sink_ref, q_ref, kp_ref, km_ref, kn_ref, vp_ref, vm_ref, vn_ref,
                    o_ref, kcat_ref, vcat_ref):
    kv = pl.program_id(1)
    t = pl.program_id(2)
    tq = q_ref.shape[0]
    blk = A_WINDOW
    kcat_ref[0:blk, :] = kp_ref[...]
    kcat_ref[blk:blk + tq, :] = km_ref[...]
    kcat_ref[blk + tq:, :] = kn_ref[...]
    vcat_ref[0:blk, :] = vp_ref[...]
    vcat_ref[blk:blk + tq, :] = vm_ref[...]
    vcat_ref[blk + tq:, :] = vn_ref[...]

    rows = A_GROUP * blk
    row = lax.broadcasted_iota(jnp.int32, (rows, 3 * blk), 0)
    col = lax.broadcasted_iota(jnp.int32, (rows, 3 * blk), 1)
    rel = (col - blk) - (row % blk)
    band = (rel <= A_WINDOW) & (rel >= -A_WINDOW)
    sink = jnp.concatenate(
        [jnp.full((blk, 1), sink_ref[kv * A_GROUP + g], F32) for g in range(A_GROUP)], axis=0)

    def body(sub, carry):
        q0 = pl.multiple_of(sub * blk, blk)
        q3 = q_ref[pl.ds(q0, blk), :]
        qs = jnp.concatenate([q3[:, g * HEAD_DIM:(g + 1) * HEAD_DIM] for g in range(A_GROUP)], axis=0)
        kw = kcat_ref[pl.ds(q0, 3 * blk), :]
        vw = vcat_ref[pl.ds(q0, 3 * blk), :]
        s = lax.dot_general(qs, kw, (((1,), (1,)), ((), ())), preferred_element_type=F32) * SCALE
        kbase = t * tq + q0 - blk
        valid = band & (col >= -kbase) & (col < SEQ - kbase)
        s = jnp.where(valid, s, NEG)
        m = jnp.maximum(jnp.max(s, axis=-1, keepdims=True), sink)
        p = jnp.exp(s - m)
        denom = jnp.sum(p, axis=-1, keepdims=True) + jnp.exp(sink - m)
        o = jnp.dot(p.astype(BF16), vw, preferred_element_type=F32) * (1.0 / denom)
        for g in range(A_GROUP):
            o_ref[pl.ds(q0, blk), g * HEAD_DIM:(g + 1) * HEAD_DIM] = o[g * blk:(g + 1) * blk].astype(BF16)
        return carry

    lax.fori_loop(0, tq // blk, body, 0, unroll=UNROLL_A)


def _mixer_a(sink, nat):
    tq = TQ_A
    nt = SEQ // tq
    nb = tq // A_WINDOW
    last = TOKENS // A_WINDOW - 1

    def main(c0):
        return lambda b, kv, t: (b * nt + t, c0 + kv)

    def prev(c0):
        return lambda b, kv, t: (jnp.maximum((b * nt + t) * nb - 1, 0), c0 + kv)

    def nxt(c0):
        return lambda b, kv, t: (jnp.minimum((b * nt + t + 1) * nb, last), c0 + kv)

    halo = (A_WINDOW, HEAD_DIM)
    return pl.pallas_call(
        _mixer_a_kernel,
        grid=(N_SEQ, A_KV_HEADS, nt),
        in_specs=[
            pl.BlockSpec(memory_space=pltpu.SMEM),
            pl.BlockSpec((tq, A_GROUP * HEAD_DIM), main(NAT_QA_384)),
            pl.BlockSpec(halo, prev(NAT_KA_128)),
            pl.BlockSpec((tq, HEAD_DIM), main(NAT_KA_128)),
            pl.BlockSpec(halo, nxt(NAT_KA_128)),
            pl.BlockSpec(halo, prev(NAT_VA_128)),
            pl.BlockSpec((tq, HEAD_DIM), main(NAT_VA_128)),
            pl.BlockSpec(halo, nxt(NAT_VA_128)),
        ],
        out_specs=pl.BlockSpec((tq, A_GROUP * HEAD_DIM), main(0)),
        out_shape=jax.ShapeDtypeStruct((TOKENS, A_Q_HEADS * HEAD_DIM), BF16),
        scratch_shapes=[pltpu.VMEM((tq + 2 * A_WINDOW, HEAD_DIM), BF16),
                        pltpu.VMEM((tq + 2 * A_WINDOW, HEAD_DIM), BF16)],
        compiler_params=pltpu.CompilerParams(
            dimension_semantics=("parallel", "parallel", "parallel"), vmem_limit_bytes=VMEM_LIMIT_BYTES),
        name="mixer_a",
    )(sink, nat, nat, nat, nat, nat, nat, nat)


def _mixer_b_kernel(q0_ref, k0p_ref, k0m_ref, k0n_ref, v0p_ref, v0m_ref, v0n_ref,
                    q1_ref, k1p_ref, k1m_ref, k1n_ref, v1p_ref, v1m_ref, v1n_ref,
                    q2_ref, k2p_ref, k2m_ref, k2n_ref, v2p_ref, v2m_ref, v2n_ref,
                    o_ref,
                    kb0_ref, vb0_ref, kb1_ref, vb1_ref, kb2_ref, vb2_ref,
                    onat_ref, lnat_ref, ores_ref, lres_ref):
    t = pl.program_id(1)
    tb = o_ref.shape[0]
    groups = (
        (1, q0_ref, (k0p_ref, k0m_ref, k0n_ref), (v0p_ref, v0m_ref, v0n_ref), kb0_ref, vb0_ref),
        (4, q1_ref, (k1p_ref, k1m_ref, k1n_ref), (v1p_ref, v1m_ref, v1n_ref), kb1_ref, vb1_ref),
        (16, q2_ref, (k2p_ref, k2m_ref, k2n_ref), (v2p_ref, v2m_ref, v2n_ref), kb2_ref, vb2_ref),
    )
    row = lax.broadcasted_iota(jnp.int32, (QB_B, QB_B + 2 * B_HALF), 0)
    col = lax.broadcasted_iota(jnp.int32, (QB_B, QB_B + 2 * B_HALF), 1)
    rel = (col - B_HALF) - row
    band = (rel <= B_HALF) & (rel >= -B_HALF)

    for gi, (d, q_ref, k_refs, v_refs, kb_ref, vb_ref) in enumerate(groups):
        p_len = tb // d
        sub_len = SEQ // d
        nsub = p_len // QB_B
        for src, dst in ((k_refs, kb_ref), (v_refs, vb_ref)):
            if d == 1:
                dst[0, 0:B_HALF, :] = src[0][...]
                dst[0, B_HALF:B_HALF + p_len, :] = src[1][...]
                dst[0, B_HALF + p_len:, :] = src[2][...]
            else:
                dst[:, 0:B_HALF, :] = src[0][...]
                dst[:, B_HALF:B_HALF + p_len, :] = src[1][...]
                dst[:, B_HALF + p_len:, :] = src[2][...]

        def body(it, carry, d=d, gi=gi, q_ref=q_ref, kb_ref=kb_ref, vb_ref=vb_ref,
                 p_len=p_len, sub_len=sub_len, nsub=nsub):
            r = it // nsub
            p0 = pl.multiple_of((it % nsub) * QB_B, QB_B)
            kbase = t * p_len + p0 - B_HALF
            valid = band & (col >= -kbase) & (col < sub_len - kbase)
            for h in range(2):
                cs = slice(h * HEAD_DIM, (h + 1) * HEAD_DIM)
                if d == 1:
                    q = q_ref[pl.ds(p0, QB_B), cs]
                else:
                    q = q_ref[r, pl.ds(p0, QB_B), cs]
                kw = kb_ref[r, pl.ds(p0, QB_B + 2 * B_HALF), cs]
                vw = vb_ref[r, pl.ds(p0, QB_B + 2 * B_HALF), cs]
                s = lax.dot_general(q, kw, (((1,), (1,)), ((), ())), preferred_element_type=F32) * SCALE
                s = jnp.where(valid, s, NEG)
                m = jnp.max(s, axis=-1, keepdims=True)
                p = jnp.exp(s - m)
                denom = jnp.sum(p, axis=-1, keepdims=True)
                o = jnp.dot(p.astype(BF16), vw, preferred_element_type=F32) * (1.0 / denom)
                lse = jnp.broadcast_to(m + jnp.log(denom), (QB_B, HEAD_DIM))
                if d == 1:
                    onat_ref[gi, h, pl.ds(p0, QB_B), :] = o
                    lnat_ref[gi, h, pl.ds(p0, QB_B), :] = lse
                else:
                    res0 = pl.multiple_of(r * p_len + p0, QB_B)
                    ores_ref[h, pl.ds(res0, QB_B), :] = o
                    lres_ref[h, pl.ds(res0, QB_B), :] = lse
            return carry

        lax.fori_loop(0, d * nsub, body, 0, unroll=UNROLL_B)
        if d > 1:
            for r in range(d):
                for h in range(2):
                    rows_nat = pl.ds(r, p_len, stride=d)
                    onat_ref[gi, h, rows_nat, :] = ores_ref[h, r * p_len:(r + 1) * p_len, :]
                    lnat_ref[gi, h, rows_nat, :] = lres_ref[h, r * p_len:(r + 1) * p_len, :]

    chunk = 256

    def combine(ci, carry):
        r0 = pl.multiple_of(ci * chunk, chunk)
        rows = pl.ds(r0, chunk)
        for h in range(2):
            l0, l1, l2 = lnat_ref[0, h, rows, :], lnat_ref[1, h, rows, :], lnat_ref[2, h, rows, :]
            mx = jnp.maximum(jnp.maximum(l0, l1), l2)
            e0, e1, e2 = jnp.exp(l0 - mx), jnp.exp(l1 - mx), jnp.exp(l2 - mx)
            inv = 1.0 / (e0 + e1 + e2)
            for gi, e in enumerate((e0, e1, e2)):
                lo = gi * CHUNK_W + h * HEAD_DIM
                o_ref[rows, lo:lo + HEAD_DIM] = (onat_ref[gi, h, rows, :] * (e * inv)).astype(BF16)
        return carry

    lax.fori_loop(0, tb // chunk, combine, 0)


def _mixer_b(nat, d4, d16):
    tb = TB_B
    nt = SEQ // tb
    in_specs = []
    args = []
    nh1 = tb // B_HALF
    last1 = TOKENS // B_HALF - 1
    in_specs.append(pl.BlockSpec((tb, CHUNK_W), lambda b, t: (b * nt + t, NAT_QB_256)))
    args.append(nat)
    for c0 in (NAT_KB_256, NAT_VB_256):
        in_specs += [
            pl.BlockSpec((B_HALF, CHUNK_W), lambda b, t, c0=c0: (jnp.maximum((b * nt + t) * nh1 - 1, 0), c0)),
            pl.BlockSpec((tb, CHUNK_W), lambda b, t, c0=c0: (b * nt + t, c0)),
            pl.BlockSpec((B_HALF, CHUNK_W), lambda b, t, c0=c0: (jnp.minimum((b * nt + t + 1) * nh1, last1), c0)),
        ]
        args += [nat, nat, nat]
    for d, arr in ((4, d4), (16, d16)):
        p_len = tb // d
        nh = p_len // B_HALF
        last = SEQ // d // B_HALF - 1
        in_specs.append(pl.BlockSpec((None, d, p_len, CHUNK_W), lambda b, t: (b, 0, t, 0)))
        args.append(arr)
        for c0 in (1, 2):
            in_specs += [
                pl.BlockSpec((None, d, B_HALF, CHUNK_W),
                             lambda b, t, c0=c0, nh=nh: (b, 0, jnp.maximum(t * nh - 1, 0), c0)),
                pl.BlockSpec((None, d, p_len, CHUNK_W), lambda b, t, c0=c0: (b, 0, t, c0)),
                pl.BlockSpec((None, d, B_HALF, CHUNK_W),
                             lambda b, t, c0=c0, nh=nh, last=last: (b, 0, jnp.minimum((t + 1) * nh, last), c0)),
            ]
            args += [arr, arr, arr]
    scratch = []
    for d in B_DILATIONS:
        shape = (d, tb // d + 2 * B_HALF, CHUNK_W)
        scratch += [pltpu.VMEM(shape, BF16), pltpu.VMEM(shape, BF16)]
    scratch += [pltpu.VMEM((3, 2, tb, HEAD_DIM), F32), pltpu.VMEM((3, 2, tb, HEAD_DIM), F32),
                pltpu.VMEM((2, tb, HEAD_DIM), F32), pltpu.VMEM((2, tb, HEAD_DIM), F32)]
    return pl.pallas_call(
        _mixer_b_kernel,
        grid=(N_SEQ, nt),
        in_specs=in_specs,
        out_specs=pl.BlockSpec((tb, B_HEADS * HEAD_DIM), lambda b, t: (b * nt + t, 0)),
        out_shape=jax.ShapeDtypeStruct((TOKENS, B_HEADS * HEAD_DIM), BF16),
        scratch_shapes=scratch,
        compiler_params=pltpu.CompilerParams(
            dimension_semantics=("parallel", "parallel"), vmem_limit_bytes=VMEM_LIMIT_BYTES),
        name="mixer_b",
    )(*args)


def _mixer_c_kernel(bias_ref, q_ref, kp_ref, km_ref, kn_ref, vp_ref, vm_ref, vn_ref,
                    o_ref, kcat_ref, vcat_ref):
    t = pl.program_id(1)
    tok = q_ref.shape[0]
    kcat_ref[0:tok, :] = kp_ref[...]
    kcat_ref[tok:2 * tok, :] = km_ref[...]
    kcat_ref[2 * tok:, :] = kn_ref[...]
    vcat_ref[0:tok, :] = vp_ref[...]
    vcat_ref[tok:2 * tok, :] = vm_ref[...]
    vcat_ref[2 * tok:, :] = vn_ref[...]

    nq = QROWS_C * GRID_W
    nk = KROWS_C * GRID_W
    row = lax.broadcasted_iota(jnp.int32, (nq, nk), 0)
    col = lax.broadcasted_iota(jnp.int32, (nq, nk), 1)
    qi, qc = row // GRID_W, row % GRID_W
    kj, kc = col // GRID_W, col % GRID_W
    cstart = jnp.clip(qc - NA_COLS // 2, 0, GRID_W - NA_COLS)
    col_valid = (kc >= cstart) & (kc < cstart + NA_COLS)
    for j in range(ROWS_C // QROWS_C):
        r = t * ROWS_C + j * QROWS_C
        first = jnp.clip(r + qi - NA_ROWS // 2, 0, GRID_ROWS - NA_ROWS)
        krel = (r - NA_ROWS // 2) + kj - first
        valid = col_valid & (krel >= 0) & (krel < NA_ROWS)
        k0 = (ROWS_C + j * QROWS_C - NA_ROWS // 2) * GRID_W
        for h in range(C_HEADS):
            cs = slice(h * HEAD_DIM, (h + 1) * HEAD_DIM)
            q = q_ref[j * nq:(j + 1) * nq, cs]
            kw = kcat_ref[k0:k0 + nk, cs]
            vw = vcat_ref[k0:k0 + nk, cs]
            s = lax.dot_general(q, kw, (((1,), (1,)), ((), ())), preferred_element_type=F32) * SCALE
            s = jnp.where(valid, s + bias_ref[h], NEG)
            m = jnp.max(s, axis=-1, keepdims=True)
            p = jnp.exp(s - m)
            denom = jnp.sum(p, axis=-1, keepdims=True)
            o = jnp.dot(p.astype(BF16), vw, preferred_element_type=F32) * (1.0 / denom)
            o_ref[j * nq:(j + 1) * nq, cs] = o.astype(BF16)


def _mixer_c(bias, nat):
    tok = TOK_C
    nt = SEQ // tok
    last = TOKENS // tok - 1
    width = C_HEADS * HEAD_DIM

    def main(c0):
        return lambda b, t: (b * nt + t, c0)

    def prev(c0):
        return lambda b, t: (jnp.maximum(b * nt + t - 1, 0), c0)

    def nxt(c0):
        return lambda b, t: (jnp.minimum(b * nt + t + 1, last), c0)

    blk = (tok, width)
    return pl.pallas_call(
        _mixer_c_kernel,
        grid=(N_SEQ, nt),
        in_specs=[
            pl.BlockSpec(bias.shape, lambda b, t: (0, 0, 0)),
            pl.BlockSpec(blk, main(NAT_QC_512)),
            pl.BlockSpec(blk, prev(NAT_KC_512)), pl.BlockSpec(blk, main(NAT_KC_512)), pl.BlockSpec(blk, nxt(NAT_KC_512)),
            pl.BlockSpec(blk, prev(NAT_VC_512)), pl.BlockSpec(blk, main(NAT_VC_512)), pl.BlockSpec(blk, nxt(NAT_VC_512)),
        ],
        out_specs=pl.BlockSpec(blk, main(0)),
        out_shape=jax.ShapeDtypeStruct((TOKENS, width), BF16),
        scratch_shapes=[pltpu.VMEM((3 * tok, width), BF16), pltpu.VMEM((3 * tok, width), BF16)],
        compiler_params=pltpu.CompilerParams(
            dimension_semantics=("parallel", "parallel"), vmem_limit_bytes=VMEM_LIMIT_BYTES),
        name="mixer_c",
    )(bias, nat, nat, nat, nat, nat, nat, nat)


def _outproj_kernel(bounds, *refs):
    nx = len(bounds)
    x_refs = refs[:nx]
    oa_ref, ob_ref, oc_ref, w_ref, g_ref, x1_ref, hn_ref, mixed_ref = refs[nx:]
    wa = oa_ref.shape[1]
    wb = ob_ref.shape[1]
    mixed_ref[:, 0:wa] = oa_ref[...]
    mixed_ref[:, wa:wa + wb] = ob_ref[...]
    mixed_ref[:, wa + wb:] = oc_ref[...]
    if nx == 1:
        res_ref = x_refs[0]
    else:
        def stage(x_ref):
            x1_ref[...] = x_ref[...]

        _for_owner(bounds, x_refs, stage)
        res_ref = x1_ref
    nc = 512
    ss = jnp.zeros((x1_ref.shape[0], 1), F32)
    for c in range(D_MODEL // nc):
        cs = slice(c * nc, (c + 1) * nc)
        y = res_ref[:, cs] + jnp.dot(mixed_ref[...], w_ref[:, cs], preferred_element_type=F32)
        x1_ref[:, cs] = y
        ss = ss + jnp.sum(y * y, axis=-1, keepdims=True)
    inv = lax.rsqrt(ss * (1.0 / D_MODEL) + EPS)
    hn_ref[...] = (x1_ref[...] * inv * g_ref[...]).astype(BF16)


def _outproj(xs, oa, ob, oc, w, g):
    tm = TM_PROJ
    const = lambda i: (0, 0)
    row = lambda i: (i, 0)
    bounds = _row_tiles(xs, tm)
    return pl.pallas_call(
        functools.partial(_outproj_kernel, bounds),
        grid=(TOKENS // tm,),
        in_specs=[_stacked_spec(b, tm, D_MODEL, 1) for b in bounds] + [
            pl.BlockSpec((tm, oa.shape[1]), row),
            pl.BlockSpec((tm, ob.shape[1]), row),
            pl.BlockSpec((tm, oc.shape[1]), row),
            pl.BlockSpec((MIX_WIDTH, D_MODEL), const, pipeline_mode=pl.Buffered(1)),
            pl.BlockSpec((1, D_MODEL), const),
        ],
        out_specs=[pl.BlockSpec((tm, D_MODEL), row), pl.BlockSpec((tm, D_MODEL), row)],
        out_shape=[jax.ShapeDtypeStruct((TOKENS, D_MODEL), F32),
                   jax.ShapeDtypeStruct((TOKENS, D_MODEL), BF16)],
        scratch_shapes=[pltpu.VMEM((tm, MIX_WIDTH), BF16)],
        compiler_params=pltpu.CompilerParams(
            dimension_semantics=("parallel",), vmem_limit_bytes=VMEM_LIMIT_BYTES),
        name="outproj",
    )(*xs, oa, ob, oc, w, g)


def _ffn_kernel(bounds, hn_ref, x1_ref, wg_ref, wu_ref, wd_ref, *refs):
    o_refs = refs[:len(bounds)]
    acc_ref = o_refs[0] if len(bounds) == 1 else refs[len(bounds)]
    f = pl.program_id(1)

    @pl.when(f == 0)
    def _():
        acc_ref[...] = x1_ref[...]

    hn = hn_ref[...]
    g = jnp.dot(hn, wg_ref[...], preferred_element_type=F32)
    u = jnp.dot(hn, wu_ref[...], preferred_element_type=F32)
    h = (g * jax.nn.sigmoid(g) * u).astype(BF16)
    nc = 512
    for c in range(D_MODEL // nc):
        cs = slice(c * nc, (c + 1) * nc)
        acc_ref[:, cs] += jnp.dot(h, wd_ref[:, cs], preferred_element_type=F32)

    if len(bounds) > 1:
        def emit(o_ref):
            o_ref[...] = acc_ref[...]

        @pl.when(f == pl.num_programs(1) - 1)
        def _():
            _for_owner(bounds, o_refs, emit)


def _ffn(hn, x1, wg, wu, wd, out_rows):
    tm, tf = TM_FFN, TF_FFN
    row = lambda i, f: (i, 0)
    outs = [jax.ShapeDtypeStruct((r, D_MODEL), F32) for r in out_rows]
    bounds = _row_tiles(outs, tm)
    return pl.pallas_call(
        functools.partial(_ffn_kernel, bounds),
        grid=(TOKENS // tm, D_FF // tf),
        in_specs=[
            pl.BlockSpec((tm, D_MODEL), row),
            pl.BlockSpec((tm, D_MODEL), row),
            pl.BlockSpec((D_MODEL, tf), lambda i, f: (0, f)),
            pl.BlockSpec((D_MODEL, tf), lambda i, f: (0, f)),
            pl.BlockSpec((tf, D_MODEL), lambda i, f: (f, 0)),
        ],
        out_specs=[_stacked_spec(b, tm, D_MODEL, 2) for b in bounds],
        out_shape=outs,
        scratch_shapes=[pltpu.VMEM((tm, D_MODEL), F32)] if len(outs) > 1 else [],
        compiler_params=pltpu.CompilerParams(
            dimension_semantics=("arbitrary", "arbitrary"), vmem_limit_bytes=VMEM_LIMIT_BYTES),
        name="ffn",
    )(hn, x1, wg, wu, wd)


def _rope_tables():
    inv = jnp.asarray((ROPE_THETA ** (-np.arange(0, ROT_DIM, 2, dtype=np.float32) / ROT_DIM)).astype(np.float32))
    ang = jnp.arange(SEQ, dtype=F32)[:, None] * inv[None, :]
    cos, sin = jnp.cos(ang), jnp.sin(ang)
    return jnp.concatenate([cos, cos, -sin, sin, jnp.zeros((SEQ, HEAD_DIM - 2 * ROT_DIM), F32)], axis=1)


def _neighbourhood_bias(rpb):
    pad = GRID_W - NA_COLS
    p = jnp.pad(rpb.astype(F32), ((0, 0), (0, 0), (pad, pad)), mode="edge")
    cols = jnp.stack([p[:, :, GRID_W - 1 - c:2 * GRID_W - 1 - c] for c in range(GRID_W)], axis=2)
    lo = NA_ROWS // 2 - 1
    b = jnp.stack([cols[:, lo - qi:lo - qi + KROWS_C] for qi in range(QROWS_C)], axis=1)
    return jnp.transpose(b, (0, 1, 3, 2, 4)).reshape(C_HEADS, QROWS_C * GRID_W, KROWS_C * GRID_W)


def kernel(x_prompt, x_sample, norm_mix, w_in, qk_norm, sink_a, rpb_c, w_out, norm_ffn, w_gate, w_up, w_down):
    xs = (x_prompt.reshape(-1, D_MODEL), x_sample.reshape(-1, D_MODEL))
    out_rows = [(TOKENS,)] * (DEPTH - 1) + [tuple(x.shape[0] for x in xs)]
    rope = _rope_tables()
    for l in range(DEPTH):
        nat, d4, d16 = _inproj(xs, norm_mix[l][None, :], w_in[l].astype(BF16),
                               qk_norm[l].reshape(6, HEAD_DIM).astype(F32), rope)
        oa = _mixer_a(sink_a[l].astype(F32), nat)
        ob = _mixer_b(nat, d4, d16)
        oc = _mixer_c(_neighbourhood_bias(rpb_c[l]), nat)
        x1, hn = _outproj(xs, oa, ob, oc, w_out[l].astype(BF16), norm_ffn[l][None, :])
        xs = tuple(_ffn(hn, x1, w_gate[l].astype(BF16), w_up[l].astype(BF16), w_down[l].astype(BF16),
                        out_rows[l]))
    return (xs[0].reshape(x_prompt.shape), xs[1].reshape(x_sample.shape))
```

```python
import functools

import jax
import jax.numpy as jnp
import numpy as np
from jax import lax
from jax.experimental import pallas as pl
from jax.experimental.pallas import tpu as pltpu

D_MODEL = 2048
SEQ = 8192
N_SEQ = 3
TOKENS = N_SEQ * SEQ
DEPTH = 2
HEAD_DIM = 128
A_Q_HEADS = 6
A_KV_HEADS = 2
A_GROUP = A_Q_HEADS // A_KV_HEADS
A_WINDOW = 128
B_DILATIONS = (1, 4, 16)
B_HALF = 64
B_HEADS = 6
C_HEADS = 4
GRID_W = 64
GRID_ROWS = SEQ // GRID_W
NA_ROWS = 8
NA_COLS = 16
MIX_WIDTH = (A_Q_HEADS + B_HEADS + C_HEADS) * HEAD_DIM
IN_WIDTH = (A_Q_HEADS + 2 * A_KV_HEADS + 3 * B_HEADS + 3 * C_HEADS) * HEAD_DIM
D_FF = 5632
ROT_DIM = HEAD_DIM // 4
ROPE_THETA = 500000.0
EPS = 1e-6
NEG = -1e30
SCALE = HEAD_DIM ** -0.5

BF16 = jnp.bfloat16
F32 = jnp.float32

VMEM_LIMIT_BYTES = 56 * 1024 * 1024

NAT_SLOTS = 28
NAT_WIDTH = NAT_SLOTS * HEAD_DIM
_CHUNKS = (
    (0, "rope", "nat", 0, (0, 0)), (2, "rope", "nat", 2, (0, 0)), (4, "rope", "nat", 4, (0, 0)),
    (6, "rope", "nat", 6, (0, 1)),
    (8, "plain", "nat", 8, None),
    (10, "rope", "nat", 10, (1, 0)), (16, "rope", "nat", 12, (1, 1)), (22, "plain", "nat", 14, None),
    (28, "norm", "nat", 16, (2, 0)), (30, "norm", "nat", 18, (2, 0)),
    (32, "norm", "nat", 20, (2, 1)), (34, "norm", "nat", 22, (2, 1)),
    (36, "plain", "nat", 24, None), (38, "plain", "nat", 26, None),
    (12, "rope", "d4", 0, (1, 0)), (18, "rope", "d4", 2, (1, 1)), (24, "plain", "d4", 4, None),
    (14, "rope", "d16", 0, (1, 0)), (20, "rope", "d16", 2, (1, 1)), (26, "plain", "d16", 4, None),
)
CHUNK_W = 2 * HEAD_DIM
NAT_QA_384 = 0
NAT_KA_128 = 6
NAT_VA_128 = 8
NAT_QB_256 = 5
NAT_KB_256 = 6
NAT_VB_256 = 7
NAT_QC_512 = 4
NAT_KC_512 = 5
NAT_VC_512 = 6

TM_PROJ = 512
TM_FFN = 512
TF_FFN = 512
TQ_A = 1024
TB_B = 2048
QB_B = 128
UNROLL_A = 4
UNROLL_B = 8
ROWS_C = 8
QROWS_C = 4
KROWS_C = QROWS_C + NA_ROWS
TOK_C = ROWS_C * GRID_W


def _rms_scale(y):
    return lax.rsqrt(jnp.mean(y * y, axis=-1, keepdims=True) + EPS)


def _row_tiles(arrays, tm):
    bounds, lo = [], 0
    for a in arrays:
        bounds.append((lo, lo + a.shape[0] // tm))
        lo = bounds[-1][1]
    return tuple(bounds)


def _stacked_spec(bounds_k, tm, width, grid_rank):
    lo, hi = bounds_k
    if grid_rank == 1:
        return pl.BlockSpec((tm, width), lambda i: (jnp.clip(i - lo, 0, hi - lo - 1), 0))
    return pl.BlockSpec((tm, width), lambda i, f: (jnp.clip(i - lo, 0, hi - lo - 1), 0))


def _for_owner(bounds, refs, fn):
    if len(refs) == 1:
        fn(refs[0])
        return
    i = pl.program_id(0)
    for (lo, hi), ref in zip(bounds, refs):
        pl.when((i >= lo) & (i < hi))(functools.partial(fn, ref))


def _inproj_kernel(bounds, *refs):
    nx = len(bounds)
    x_refs = refs[:nx]
    g_ref, w_ref, gain_ref, rope_ref, nat_ref, d4_ref, d16_ref, hn_ref, ybuf_ref = refs[nx:]

    def prenorm(x_ref):
        x = x_ref[...]
        hn_ref[...] = (x * _rms_scale(x) * g_ref[...]).astype(BF16)

    _for_owner(bounds, x_refs, prenorm)
    tm = hn_ref.shape[0]
    half = ROT_DIM // 2
    table = rope_ref[...]
    lane = lax.broadcasted_iota(jnp.int32, table.shape, 1)
    sin_pair = pltpu.roll(table, HEAD_DIM - ROT_DIM, 1)
    rope_c = jnp.where(lane < ROT_DIM, table, 1.0)
    rope_sa = jnp.where(lane < half, sin_pair, 0.0)
    rope_sb = jnp.where((lane >= half) & (lane < ROT_DIM), sin_pair, 0.0)
    for h0, kind, dest, slot, gidx in _CHUNKS:
        acc = jnp.dot(hn_ref[...], w_ref[:, h0 * HEAD_DIM:(h0 + 2) * HEAD_DIM],
                      preferred_element_type=F32)
        for h in range(2):
            y = acc[:, h * HEAD_DIM:(h + 1) * HEAD_DIM]
            if kind != "plain":
                gi = 2 * gidx[0] + gidx[1]
                y = y * _rms_scale(y) * gain_ref[gi:gi + 1, :]
            if kind == "rope":
                y = (y * rope_c + pltpu.roll(y, HEAD_DIM - half, 1) * rope_sa
                     + pltpu.roll(y, half, 1) * rope_sb)
            lo = (slot + h) * HEAD_DIM
            if dest == "nat":
                nat_ref[:, lo:lo + HEAD_DIM] = y.astype(BF16)
            else:
                d, out_ref = (4, d4_ref) if dest == "d4" else (16, d16_ref)
                ybuf_ref[...] = y
                for r in range(d):
                    out_ref[r, :, lo:lo + HEAD_DIM] = ybuf_ref[pl.ds(r, tm // d, stride=d), :].astype(BF16)


def _inproj(xs, g, w, gain, rope):
    tm = TM_PROJ
    nt = SEQ // tm
    const = lambda i: (0, 0)
    bounds = _row_tiles(xs, tm)
    return pl.pallas_call(
        functools.partial(_inproj_kernel, bounds),
        grid=(TOKENS // tm,),
        in_specs=[_stacked_spec(b, tm, D_MODEL, 1) for b in bounds] + [
            pl.BlockSpec((1, D_MODEL), const),
            pl.BlockSpec((D_MODEL, IN_WIDTH), const, pipeline_mode=pl.Buffered(1)),
            pl.BlockSpec((6, HEAD_DIM), const),
            pl.BlockSpec((tm, HEAD_DIM), lambda i: (i % nt, 0)),
        ],
        out_specs=[
            pl.BlockSpec((tm, NAT_WIDTH), lambda i: (i, 0)),
            pl.BlockSpec((None, 4, tm // 4, 3 * CHUNK_W), lambda i: (i // nt, 0, i % nt, 0)),
            pl.BlockSpec((None, 16, tm // 16, 3 * CHUNK_W), lambda i: (i // nt, 0, i % nt, 0)),
        ],
        out_shape=[
            jax.ShapeDtypeStruct((TOKENS, NAT_WIDTH), BF16),
            jax.ShapeDtypeStruct((N_SEQ, 4, SEQ // 4, 3 * CHUNK_W), BF16),
            jax.ShapeDtypeStruct((N_SEQ, 16, SEQ // 16, 3 * CHUNK_W), BF16),
        ],
        scratch_shapes=[pltpu.VMEM((tm, D_MODEL), BF16), pltpu.VMEM((tm, HEAD_DIM), F32)],
        compiler_params=pltpu.CompilerParams(
            dimension_semantics=("parallel",), vmem_limit_bytes=VMEM_LIMIT_BYTES),
        name="inproj",
    )(*xs, g, w, gain, rope)


def _mixer_a_kernel(sink_ref, q_ref, kp_ref, km_ref, kn_ref, vp_ref, vm_ref, vn_ref,
                    o_ref, kcat_ref, vcat_ref):
    kv = pl.program_id(1)
    t = pl.program_id(2)
    tq = q_ref.shape[0]
    blk = A_WINDOW
    kcat_ref[0:blk, :] = kp_ref[...]
    kcat_ref[blk:blk + tq, :] = km_ref[...]
    kcat_ref[blk + tq:, :] = kn_ref[...]
    vcat_ref[0:blk, :] = vp_ref[...]
    vcat_ref[blk:blk + tq, :] = vm_ref[...]
    vcat_ref[blk + tq:, :] = vn_ref[...]

    rows = A_GROUP * blk
    row = lax.broadcasted_iota(jnp.int32, (rows, 3 * blk), 0)
    col = lax.broadcasted_iota(jnp.int32, (rows, 3 * blk), 1)
    rel = (col - blk) - (row % blk)
    band = (rel <= A_WINDOW) & (rel >= -A_WINDOW)
    sink = jnp.concatenate(
        [jnp.full((blk, 1), sink_ref[kv * A_GROUP + g], F32) for g in range(A_GROUP)], axis=0)

    def body(sub, carry):
        q0 = pl.multiple_of(sub * blk, blk)
        q3 = q_ref[pl.ds(q0, blk), :]
        qs = jnp.concatenate([q3[:, g * HEAD_DIM:(g + 1) * HEAD_DIM] for g in range(A_GROUP)], axis=0)
        kw = kcat_ref[pl.ds(q0, 3 * blk), :]
        vw = vcat_ref[pl.ds(q0, 3 * blk), :]
        s = lax.dot_general(qs, kw, (((1,), (1,)), ((), ())), preferred_element_type=F32) * SCALE
        kbase = t * tq + q0 - blk
        valid = band & (col >= -kbase) & (col < SEQ - kbase)
        s = jnp.where(valid, s, NEG)
        m = jnp.maximum(jnp.max(s, axis=-1, keepdims=True), sink)
        p = jnp.exp(s - m)
        denom = jnp.sum(p, axis=-1, keepdims=True) + jnp.exp(sink - m)
        o = jnp.dot(p.astype(BF16), vw, preferred_element_type=F32) * (1.0 / denom)
        for g in range(A_GROUP):
            o_ref[pl.ds(q0, blk), g * HEAD_DIM:(g + 1) * HEAD_DIM] = o[g * blk:(g + 1) * blk].astype(BF16)
        return carry

    lax.fori_loop(0, tq // blk, body, 0, unroll=UNROLL_A)


def _mixer_a(sink, nat):
    tq = TQ_A
    nt = SEQ // tq
    nb = tq // A_WINDOW
    last = TOKENS // A_WINDOW - 1

    def main(c0):
        return lambda b, kv, t: (b * nt + t, c0 + kv)

    def prev(c0):
        return lambda b, kv, t: (jnp.maximum((b * nt + t) * nb - 1, 0), c0 + kv)

    def nxt(c0):
        return lambda b, kv, t: (jnp.minimum((b * nt + t + 1) * nb, last), c0 + kv)

    halo = (A_WINDOW, HEAD_DIM)
    return pl.pallas_call(
        _mixer_a_kernel,
        grid=(N_SEQ, A_KV_HEADS, nt),
        in_specs=[
            pl.BlockSpec(memory_space=pltpu.SMEM),
            pl.BlockSpec((tq, A_GROUP * HEAD_DIM), main(NAT_QA_384)),
            pl.BlockSpec(halo, prev(NAT_KA_128)),
            pl.BlockSpec((tq, HEAD_DIM), main(NAT_KA_128)),
            pl.BlockSpec(halo, nxt(NAT_KA_128)),
            pl.BlockSpec(halo, prev(NAT_VA_128)),
            pl.BlockSpec((tq, HEAD_DIM), main(NAT_VA_128)),
            pl.BlockSpec(halo, nxt(NAT_VA_128)),
        ],
        out_specs=pl.BlockSpec((tq, A_GROUP * HEAD_DIM), main(0)),
        out_shape=jax.ShapeDtypeStruct((TOKENS, A_Q_HEADS * HEAD_DIM), BF16),
        scratch_shapes=[pltpu.VMEM((tq + 2 * A_WINDOW, HEAD_DIM), BF16),
                        pltpu.VMEM((tq + 2 * A_WINDOW, HEAD_DIM), BF16)],
        compiler_params=pltpu.CompilerParams(
            dimension_semantics=("parallel", "parallel", "parallel"), vmem_limit_bytes=VMEM_LIMIT_BYTES),
        name="mixer_a",
    )(sink, nat, nat, nat, nat, nat, nat, nat)


def _mixer_b_kernel(q0_ref, k0p_ref, k0m_ref, k0n_ref, v0p_ref, v0m_ref, v0n_ref,
                    q1_ref, k1p_ref, k1m_ref, k1n_ref, v1p_ref, v1m_ref, v1n_ref,
                    q2_ref, k2p_ref, k2m_ref, k2n_ref, v2p_ref, v2m_ref, v2n_ref,
                    o_ref,
                    kb0_ref, vb0_ref, kb1_ref, vb1_ref, kb2_ref, vb2_ref,
                    onat_ref, lnat_ref, ores_ref, lres_ref):
    t = pl.program_id(1)
    tb = o_ref.shape[0]
    groups = (
        (1, q0_ref, (k0p_ref, k0m_ref, k0n_ref), (v0p_ref, v0m_ref, v0n_ref), kb0_ref, vb0_ref),
        (4, q1_ref, (k1p_ref, k1m_ref, k1n_ref), (v1p_ref, v1m_ref, v1n_ref), kb1_ref, vb1_ref),
        (16, q2_ref, (k2p_ref, k2m_ref, k2n_ref), (v2p_ref, v2m_ref, v2n_ref), kb2_ref, vb2_ref),
    )
    row = lax.broadcasted_iota(jnp.int32, (QB_B, QB_B + 2 * B_HALF), 0)
    col = lax.broadcasted_iota(jnp.int32, (QB_B, QB_B + 2 * B_HALF), 1)
    rel = (col - B_HALF) - row
    band = (rel <= B_HALF) & (rel >= -B_HALF)

    for gi, (d, q_ref, k_refs, v_refs, kb_ref, vb_ref) in enumerate(groups):
        p_len = tb // d
        sub_len = SEQ // d
        nsub = p_len // QB_B
        for src, dst in ((k_refs, kb_ref), (v_refs, vb_ref)):
            if d == 1:
                dst[0, 0:B_HALF, :] = src[0][...]
                dst[0, B_HALF:B_HALF + p_len, :] = src[1][...]
                dst[0, B_HALF + p_len:, :] = src[2][...]
            else:
                dst[:, 0:B_HALF, :] = src[0][...]
                dst[:, B_HALF:B_HALF + p_len, :] = src[1][...]
                dst[:, B_HALF + p_len:, :] = src[2][...]

        def body(it, carry, d=d, gi=gi, q_ref=q_ref, kb_ref=kb_ref, vb_ref=vb_ref,
                 p_len=p_len, sub_len=sub_len, nsub=nsub):
            r = it // nsub
            p0 = pl.multiple_of((it % nsub) * QB_B, QB_B)
            kbase = t * p_len + p0 - B_HALF
            valid = band & (col >= -kbase) & (col < sub_len - kbase)
            for h in range(2):
                cs = slice(h * HEAD_DIM, (h + 1) * HEAD_DIM)
                if d == 1:
                    q = q_ref[pl.ds(p0, QB_B), cs]
                else:
                    q = q_ref[r, pl.ds(p0, QB_B), cs]
                kw = kb_ref[r, pl.ds(p0, QB_B + 2 * B_HALF), cs]
                vw = vb_ref[r, pl.ds(p0, QB_B + 2 * B_HALF), cs]
                s = lax.dot_general(q, kw, (((1,), (1,)), ((), ())), preferred_element_type=F32) * SCALE
                s = jnp.where(valid, s, NEG)
                m = jnp.max(s, axis=-1, keepdims=True)
                p = jnp.exp(s - m)
                denom = jnp.sum(p, axis=-1, keepdims=True)
                o = jnp.dot(p.astype(BF16), vw, preferred_element_type=F32) * (1.0 / denom)
                lse = jnp.broadcast_to(m + jnp.log(denom), (QB_B, HEAD_DIM))
                if d == 1:
                    onat_ref[gi, h, pl.ds(p0, QB_B), :] = o
                    lnat_ref[gi, h, pl.ds(p0, QB_B), :] = lse
                else:
                    res0 = pl.multiple_of(r * p_len + p0, QB_B)
                    ores_ref[h, pl.ds(res0, QB_B), :] = o
                    lres_ref[h, pl.ds(res0, QB_B), :] = lse
            return carry

        lax.fori_loop(0, d * nsub, body, 0, unroll=UNROLL_B)
        if d > 1:
            for r in range(d):
                for h in range(2):
                    rows_nat = pl.ds(r, p_len, stride=d)
                    onat_ref[gi, h, rows_nat, :] = ores_ref[h, r * p_len:(r + 1) * p_len, :]
                    lnat_ref[gi, h, rows_nat, :] = lres_ref[h, r * p_len:(r + 1) * p_len, :]

    chunk = 256

    def combine(ci, carry):
        r0 = pl.multiple_of(ci * chunk, chunk)
        rows = pl.ds(r0, chunk)
        for h in range(2):
            l0, l1, l2 = lnat_ref[0, h, rows, :], lnat_ref[1, h, rows, :], lnat_ref[2, h, rows, :]
            mx = jnp.maximum(jnp.maximum(l0, l1), l2)
            e0, e1, e2 = jnp.exp(l0 - mx), jnp.exp(l1 - mx), jnp.exp(l2 - mx)
            inv = 1.0 / (e0 + e1 + e2)
            for gi, e in enumerate((e0, e1, e2)):
                lo = gi * CHUNK_W + h * HEAD_DIM
                o_ref[rows, lo:lo + HEAD_DIM] = (onat_ref[gi, h, rows, :] * (e * inv)).astype(BF16)
        return carry

    lax.fori_loop(0, tb // chunk, combine, 0)


def _mixer_b(nat, d4, d16):
    tb = TB_B
    nt = SEQ // tb
    in_specs = []
    args = []
    nh1 = tb // B_HALF
    last1 = TOKENS // B_HALF - 1
    in_specs.append(pl.BlockSpec((tb, CHUNK_W), lambda b, t: (b * nt + t, NAT_QB_256)))
    args.append(nat)
    for c0 in (NAT_KB_256, NAT_VB_256):
        in_specs += [
            pl.BlockSpec((B_HALF, CHUNK_W), lambda b, t, c0=c0: (jnp.maximum((b * nt + t) * nh1 - 1, 0), c0)),
            pl.BlockSpec((tb, CHUNK_W), lambda b, t, c0=c0: (b * nt + t, c0)),
            pl.BlockSpec((B_HALF, CHUNK_W), lambda b, t, c0=c0: (jnp.minimum((b * nt + t + 1) * nh1, last1), c0)),
        ]
        args += [nat, nat, nat]
    for d, arr in ((4, d4), (16, d16)):
        p_len = tb // d
        nh = p_len // B_HALF
        last = SEQ // d // B_HALF - 1
        in_specs.append(pl.BlockSpec((None, d, p_len, CHUNK_W), lambda b, t: (b, 0, t, 0)))
        args.append(arr)
        for c0 in (1, 2):
            in_specs += [
                pl.BlockSpec((None, d, B_HALF, CHUNK_W),
                             lambda b, t, c0=c0, nh=nh: (b, 0, jnp.maximum(t * nh - 1, 0), c0)),
                pl.BlockSpec((None, d, p_len, CHUNK_W), lambda b, t, c0=c0: (b, 0, t, c0)),
                pl.BlockSpec((None, d, B_HALF, CHUNK_W),
                             lambda b, t, c0=c0, nh=nh, last=last: (b, 0, jnp.minimum((t + 1) * nh, last), c0)),
            ]
            args += [arr, arr, arr]
    scratch = []
    for d in B_DILATIONS:
        shape = (d, tb // d + 2 * B_HALF, CHUNK_W)
        scratch += [pltpu.VMEM(shape, BF16), pltpu.VMEM(shape, BF16)]
    scratch += [pltpu.VMEM((3, 2, tb, HEAD_DIM), F32), pltpu.VMEM((3, 2, tb, HEAD_DIM), F32),
                pltpu.VMEM((2, tb, HEAD_DIM), F32), pltpu.VMEM((2, tb, HEAD_DIM), F32)]
    return pl.pallas_call(
        _mixer_b_kernel,
        grid=(N_SEQ, nt),
        in_specs=in_specs,
        out_specs=pl.BlockSpec((tb, B_HEADS * HEAD_DIM), lambda b, t: (b * nt + t, 0)),
        out_shape=jax.ShapeDtypeStruct((TOKENS, B_HEADS * HEAD_DIM), BF16),
        scratch_shapes=scratch,
        compiler_params=pltpu.CompilerParams(
            dimension_semantics=("parallel", "parallel"), vmem_limit_bytes=VMEM_LIMIT_BYTES),
        name="mixer_b",
    )(*args)


def _mixer_c_kernel(bias_ref, q_ref, kp_ref, km_ref, kn_ref, vp_ref, vm_ref, vn_ref,
                    o_ref, kcat_ref, vcat_ref):
    t = pl.program_id(1)
    tok = q_ref.shape[0]
    kcat_ref[0:tok, :] = kp_ref[...]
    kcat_ref[tok:2 * tok, :] = km_ref[...]
    kcat_ref[2 * tok:, :] = kn_ref[...]
    vcat_ref[0:tok, :] = vp_ref[...]
    vcat_ref[tok:2 * tok, :] = vm_ref[...]
    vcat_ref[2 * tok:, :] = vn_ref[...]

    nq = QROWS_C * GRID_W
    nk = KROWS_C * GRID_W
    row = lax.broadcasted_iota(jnp.int32, (nq, nk), 0)
    col = lax.broadcasted_iota(jnp.int32, (nq, nk), 1)
    qi, qc = row // GRID_W, row % GRID_W
    kj, kc = col // GRID_W, col % GRID_W
    cstart = jnp.clip(qc - NA_COLS // 2, 0, GRID_W - NA_COLS)
    col_valid = (kc >= cstart) & (kc < cstart + NA_COLS)
    for j in range(ROWS_C // QROWS_C):
        r = t * ROWS_C + j * QROWS_C
        first = jnp.clip(r + qi - NA_ROWS // 2, 0, GRID_ROWS - NA_ROWS)
        krel = (r - NA_ROWS // 2) + kj - first
        valid = col_valid & (krel >= 0) & (krel < NA_ROWS)
        k0 = (ROWS_C + j * QROWS_C - NA_ROWS // 2) * GRID_W
        for h in range(C_HEADS):
            cs = slice(h * HEAD_DIM, (h + 1) * HEAD_DIM)
            q = q_ref[j * nq:(j + 1) * nq, cs]
            kw = kcat_ref[k0:k0 + nk, cs]
            vw = vcat_ref[k0:k0 + nk, cs]
            s = lax.dot_general(q, kw, (((1,), (1,)), ((), ())), preferred_element_type=F32) * SCALE
            s = jnp.where(valid, s + bias_ref[h], NEG)
            m = jnp.max(s, axis=-1, keepdims=True)
            p = jnp.exp(s - m)
            denom = jnp.sum(p, axis=-1, keepdims=True)
            o = jnp.dot(p.astype(BF16), vw, preferred_element_type=F32) * (1.0 / denom)
            o_ref[j * nq:(j + 1) * nq, cs] = o.astype(BF16)


def _mixer_c(bias, nat):
    tok = TOK_C
    nt = SEQ // tok
    last = TOKENS // tok - 1
    width = C_HEADS * HEAD_DIM

    def main(c0):
        return lambda b, t: (b * nt + t, c0)

    def prev(c0):
        return lambda b, t: (jnp.maximum(b * nt + t - 1, 0), c0)

    def nxt(c0):
        return lambda b, t: (jnp.minimum(b * nt + t + 1, last), c0)

    blk = (tok, width)
    return pl.pallas_call(
        _mixer_c_kernel,
        grid=(N_SEQ, nt),
        in_specs=[
            pl.BlockSpec(bias.shape, lambda b, t: (0, 0, 0)),
            pl.BlockSpec(blk, main(NAT_QC_512)),
            pl.BlockSpec(blk, prev(NAT_KC_512)), pl.BlockSpec(blk, main(NAT_KC_512)), pl.BlockSpec(blk, nxt(NAT_KC_512)),
            pl.BlockSpec(blk, prev(NAT_VC_512)), pl.BlockSpec(blk, main(NAT_VC_512)), pl.BlockSpec(blk, nxt(NAT_VC_512)),
        ],
        out_specs=pl.BlockSpec(blk, main(0)),
        out_shape=jax.ShapeDtypeStruct((TOKENS, width), BF16),
        scratch_shapes=[pltpu.VMEM((3 * tok, width), BF16), pltpu.VMEM((3 * tok, width), BF16)],
        compiler_params=pltpu.CompilerParams(
            dimension_semantics=("parallel", "parallel"), vmem_limit_bytes=VMEM_LIMIT_BYTES),
        name="mixer_c",
    )(bias, nat, nat, nat, nat, nat, nat, nat)


def _outproj_kernel(bounds, *refs):
    nx = len(bounds)
    x_refs = refs[:nx]
    oa_ref, ob_ref, oc_ref, w_ref, g_ref, x1_ref, hn_ref, mixed_ref = refs[nx:]
    wa = oa_ref.shape[1]
    wb = ob_ref.shape[1]
    mixed_ref[:, 0:wa] = oa_ref[...]
    mixed_ref[:, wa:wa + wb] = ob_ref[...]
    mixed_ref[:, wa + wb:] = oc_ref[...]
    if nx == 1:
        res_ref = x_refs[0]
    else:
        def stage(x_ref):
            x1_ref[...] = x_ref[...]

        _for_owner(bounds, x_refs, stage)
        res_ref = x1_ref
    nc = 512
    ss = jnp.zeros((x1_ref.shape[0], 1), F32)
    for c in range(D_MODEL // nc):
        cs = slice(c * nc, (c + 1) * nc)
        y = res_ref[:, cs] + jnp.dot(mixed_ref[...], w_ref[:, cs], preferred_element_type=F32)
        x1_ref[:, cs] = y
        ss = ss + jnp.sum(y * y, axis=-1, keepdims=True)
    inv = lax.rsqrt(ss * (1.0 / D_MODEL) + EPS)
    hn_ref[...] = (x1_ref[...] * inv * g_ref[...]).astype(BF16)


def _outproj(xs, oa, ob, oc, w, g):
    tm = TM_PROJ
    const = lambda i: (0, 0)
    row = lambda i: (i, 0)
    bounds = _row_tiles(xs, tm)
    return pl.pallas_call(
        functools.partial(_outproj_kernel, bounds),
        grid=(TOKENS // tm,),
        in_specs=[_stacked_spec(b, tm, D_MODEL, 1) for b in bounds] + [
            pl.BlockSpec((tm, oa.shape[1]), row),
            pl.BlockSpec((tm, ob.shape[1]), row),
            pl.BlockSpec((tm, oc.shape[1]), row),
            pl.BlockSpec((MIX_WIDTH, D_MODEL), const, pipeline_mode=pl.Buffered(1)),
            pl.BlockSpec((1, D_MODEL), const),
        ],
        out_specs=[pl.BlockSpec((tm, D_MODEL), row), pl.BlockSpec((tm, D_MODEL), row)],
        out_shape=[jax.ShapeDtypeStruct((TOKENS, D_MODEL), F32),
                   jax.ShapeDtypeStruct((TOKENS, D_MODEL), BF16)],
        scratch_shapes=[pltpu.VMEM((tm, MIX_WIDTH), BF16)],
        compiler_params=pltpu.CompilerParams(
            dimension_semantics=("parallel",), vmem_limit_bytes=VMEM_LIMIT_BYTES),
        name="outproj",
    )(*xs, oa, ob, oc, w, g)


def _ffn_kernel(bounds, hn_ref, x1_ref, wg_ref, wu_ref, wd_ref, *refs):
    o_refs = refs[:len(bounds)]
    acc_ref = o_refs[0] if len(bounds) == 1 else refs[len(bounds)]
    f = pl.program_id(1)

    @pl.when(f == 0)
    def _():
        acc_ref[...] = x1_ref[...]

    hn = hn_ref[...]
    g = jnp.dot(hn, wg_ref[...], preferred_element_type=F32)
    u = jnp.dot(hn, wu_ref[...], preferred_element_type=F32)
    h = (g * jax.nn.sigmoid(g) * u).astype(BF16)
    nc = 512
    for c in range(D_MODEL // nc):
        cs = slice(c * nc, (c + 1) * nc)
        acc_ref[:, cs] += jnp.dot(h, wd_ref[:, cs], preferred_element_type=F32)

    if len(bounds) > 1:
        def emit(o_ref):
            o_ref[...] = acc_ref[...]

        @pl.when(f == pl.num_programs(1) - 1)
        def _():
            _for_owner(bounds, o_refs, emit)


def _ffn(hn, x1, wg, wu, wd, out_rows):
    tm, tf = TM_FFN, TF_FFN
    row = lambda i, f: (i, 0)
    outs = [jax.ShapeDtypeStruct((r, D_MODEL), F32) for r in out_rows]
    bounds = _row_tiles(outs, tm)
    return pl.pallas_call(
        functools.partial(_ffn_kernel, bounds),
        grid=(TOKENS // tm, D_FF // tf),
        in_specs=[
            pl.BlockSpec((tm, D_MODEL), row),
            pl.BlockSpec((tm, D_MODEL), row),
            pl.BlockSpec((D_MODEL, tf), lambda i, f: (0, f)),
            pl.BlockSpec((D_MODEL, tf), lambda i, f: (0, f)),
            pl.BlockSpec((tf, D_MODEL), lambda i, f: (f, 0)),
        ],
        out_specs=[_stacked_spec(b, tm, D_MODEL, 2) for b in bounds],
        out_shape=outs,
        scratch_shapes=[pltpu.VMEM((tm, D_MODEL), F32)] if len(outs) > 1 else [],
        compiler_params=pltpu.CompilerParams(
            dimension_semantics=("arbitrary", "arbitrary"), vmem_limit_bytes=VMEM_LIMIT_BYTES),
        name="ffn",
    )(hn, x1, wg, wu, wd)


def _rope_tables():
    inv = jnp.asarray((ROPE_THETA ** (-np.arange(0, ROT_DIM, 2, dtype=np.float32) / ROT_DIM)).astype(np.float32))
    ang = jnp.arange(SEQ, dtype=F32)[:, None] * inv[None, :]
    cos, sin = jnp.cos(ang), jnp.sin(ang)
    return jnp.concatenate([cos, cos, -sin, sin, jnp.zeros((SEQ, HEAD_DIM - 2 * ROT_DIM), F32)], axis=1)


def _neighbourhood_bias(rpb):
    pad = GRID_W - NA_COLS
    p = jnp.pad(rpb.astype(F32), ((0, 0), (0, 0), (pad, pad)), mode="edge")
    cols = jnp.stack([p[:, :, GRID_W - 1 - c:2 * GRID_W - 1 - c] for c in range(GRID_W)], axis=2)
    lo = NA_ROWS // 2 - 1
    b = jnp.stack([cols[:, lo - qi:lo - qi + KROWS_C] for qi in range(QROWS_C)], axis=1)
    return jnp.transpose(b, (0, 1, 3, 2, 4)).reshape(C_HEADS, QROWS_C * GRID_W, KROWS_C * GRID_W)


def kernel(x_prompt, x_sample, norm_mix, w_in, qk_norm, sink_a, rpb_c, w_out, norm_ffn, w_gate, w_up, w_down):
    xs = (x_prompt.reshape(-1, D_MODEL), x_sample.reshape(-1, D_MODEL))
    out_rows = [(TOKENS,)] * (DEPTH - 1) + [tuple(x.shape[0] for x in xs)]
    rope = _rope_tables()
    for l in range(DEPTH):
        nat, d4, d16 = _inproj(xs, norm_mix[l][None, :], w_in[l].astype(BF16),
                               qk_norm[l].reshape(6, HEAD_DIM).astype(F32), rope)
        oa = _mixer_a(sink_a[l].astype(F32), nat)
        ob = _mixer_b(nat, d4, d16)
        oc = _mixer_c(_neighbourhood_bias(rpb_c[l]), nat)
        x1, hn = _outproj(xs, oa, ob, oc, w_out[l].astype(BF16), norm_ffn[l][None, :])
        xs = tuple(_ffn(hn, x1, w_gate[l].astype(BF16), w_up[l].astype(BF16), w_down[l].astype(BF16),
                        out_rows[l]))
    return (xs[0].reshape(x_prompt.shape), xs[1].reshape(x_sample.shape))
```

```python
import functools

import jax
import jax.numpy as jnp
import numpy as np
from jax import lax
from jax.experimental import pallas as pl
from jax.experimental.pallas import tpu as pltpu

D_MODEL = 2048
SEQ = 8192
N_SEQ = 3
TOKENS = N_SEQ * SEQ
DEPTH = 2
HEAD_DIM = 128
A_Q_HEADS = 6
A_KV_HEADS = 2
A_GROUP = A_Q_HEADS // A_KV_HEADS
A_WINDOW = 128
B_DILATIONS = (1, 4, 16)
B_HALF = 64
B_HEADS = 6
C_HEADS = 4
GRID_W = 64
GRID_ROWS = SEQ // GRID_W
NA_ROWS = 8
NA_COLS = 16
MIX_WIDTH = (A_Q_HEADS + B_HEADS + C_HEADS) * HEAD_DIM
IN_WIDTH = (A_Q_HEADS + 2 * A_KV_HEADS + 3 * B_HEADS + 3 * C_HEADS) * HEAD_DIM
D_FF = 5632
ROT_DIM = HEAD_DIM // 4
ROPE_THETA = 500000.0
EPS = 1e-6
NEG = -1e30
SCALE = HEAD_DIM ** -0.5

BF16 = jnp.bfloat16
F32 = jnp.float32

VMEM_LIMIT_BYTES = 56 * 1024 * 1024

NAT_SLOTS = 28
NAT_WIDTH = NAT_SLOTS * HEAD_DIM
_CHUNKS = (
    (14, "rope", "d16", 0, (1, 0)), (20, "rope", "d16", 2, (1, 1)), (26, "plain", "d16", 4, None),
    (12, "rope", "d4", 0, (1, 0)), (18, "rope", "d4", 2, (1, 1)), (24, "plain", "d4", 4, None),
    (0, "rope", "nat", 0, (0, 0)), (2, "rope", "nat", 2, (0, 0)), (4, "rope", "nat", 4, (0, 0)),
    (6, "rope", "nat", 6, (0, 1)),
    (10, "rope", "nat", 10, (1, 0)), (16, "rope", "nat", 12, (1, 1)),
    (28, "norm", "nat", 16, (2, 0)), (30, "norm", "nat", 18, (2, 0)),
    (32, "norm", "nat", 20, (2, 1)), (34, "norm", "nat", 22, (2, 1)),
    (8, "plain", "nat", 8, None), (22, "plain", "nat", 14, None),
    (36, "plain", "nat", 24, None), (38, "plain", "nat", 26, None),
)
CHUNK_W = 2 * HEAD_DIM
NAT_QA_384 = 0
NAT_KA_128 = 6
NAT_VA_128 = 8
NAT_QB_256 = 5
NAT_KB_256 = 6
NAT_VB_256 = 7
NAT_QC_512 = 4
NAT_KC_512 = 5
NAT_VC_512 = 6

TM_PROJ = 512
TM_FFN = 512
TF_FFN = 512
TQ_A = 1024
TB_B = 2048
QB_B = 128
UNROLL_A = 4
UNROLL_B = 8
ROWS_C = 8
QROWS_C = 4
KROWS_C = QROWS_C + NA_ROWS
TOK_C = ROWS_C * GRID_W


def _rms_scale(y):
    return lax.rsqrt(jnp.mean(y * y, axis=-1, keepdims=True) + EPS)


def _row_tiles(arrays, tm):
    bounds, lo = [], 0
    for a in arrays:
        bounds.append((lo, lo + a.shape[0] // tm))
        lo = bounds[-1][1]
    return tuple(bounds)


def _stacked_spec(bounds_k, tm, width, grid_rank):
    lo, hi = bounds_k
    if grid_rank == 1:
        return pl.BlockSpec((tm, width), lambda i: (jnp.clip(i - lo, 0, hi - lo - 1), 0))
    return pl.BlockSpec((tm, width), lambda i, f: (jnp.clip(i - lo, 0, hi - lo - 1), 0))


def _for_owner(bounds, refs, fn):
    if len(refs) == 1:
        fn(refs[0])
        return
    i = pl.program_id(0)
    for (lo, hi), ref in zip(bounds, refs):
        pl.when((i >= lo) & (i < hi))(functools.partial(fn, ref))


def _inproj_kernel(bounds, *refs):
    nx = len(bounds)
    x_refs = refs[:nx]
    g_ref, w_ref, gain_ref, rope_ref, nat_ref, d4_ref, d16_ref, hn_ref, ybuf_ref = refs[nx:]

    def prenorm(x_ref):
        x = x_ref[...]
        hn_ref[...] = (x * _rms_scale(x) * g_ref[...]).astype(BF16)

    _for_owner(bounds, x_refs, prenorm)
    tm = hn_ref.shape[0]
    half = ROT_DIM // 2
    table = rope_ref[...]
    lane = lax.broadcasted_iota(jnp.int32, table.shape, 1)
    sin_pair = pltpu.roll(table, HEAD_DIM - ROT_DIM, 1)
    rope_c = jnp.where(lane < ROT_DIM, table, 1.0)
    rope_sa = jnp.where(lane < half, sin_pair, 0.0)
    rope_sb = jnp.where((lane >= half) & (lane < ROT_DIM), sin_pair, 0.0)
    for h0, kind, dest, slot, gidx in _CHUNKS:
        acc = jnp.dot(hn_ref[...], w_ref[:, h0 * HEAD_DIM:(h0 + 2) * HEAD_DIM],
                      preferred_element_type=F32)
        for h in range(2):
            y = acc[:, h * HEAD_DIM:(h + 1) * HEAD_DIM]
            if kind != "plain":
                gi = 2 * gidx[0] + gidx[1]
                y = y * _rms_scale(y) * gain_ref[gi:gi + 1, :]
            if kind == "rope":
                y = (y * rope_c + pltpu.roll(y, HEAD_DIM - half, 1) * rope_sa
                     + pltpu.roll(y, half, 1) * rope_sb)
            lo = (slot + h) * HEAD_DIM
            if dest == "nat":
                nat_ref[:, lo:lo + HEAD_DIM] = y.astype(BF16)
            else:
                d, out_ref = (4, d4_ref) if dest == "d4" else (16, d16_ref)
                ybuf_ref[...] = y
                for r in range(d):
                    out_ref[r, :, lo:lo + HEAD_DIM] = ybuf_ref[pl.ds(r, tm // d, stride=d), :].astype(BF16)


def _inproj(xs, g, w, gain, rope):
    tm = TM_PROJ
    nt = SEQ // tm
    const = lambda i: (0, 0)
    bounds = _row_tiles(xs, tm)
    return pl.pallas_call(
        functools.partial(_inproj_kernel, bounds),
        grid=(TOKENS // tm,),
        in_specs=[_stacked_spec(b, tm, D_MODEL, 1) for b in bounds] + [
            pl.BlockSpec((1, D_MODEL), const),
            pl.BlockSpec((D_MODEL, IN_WIDTH), const, pipeline_mode=pl.Buffered(1)),
            pl.BlockSpec((6, HEAD_DIM), const),
            pl.BlockSpec((tm, HEAD_DIM), lambda i: (i % nt, 0)),
        ],
        out_specs=[
            pl.BlockSpec((tm, NAT_WIDTH), lambda i: (i, 0)),
            pl.BlockSpec((None, 4, tm // 4, 3 * CHUNK_W), lambda i: (i // nt, 0, i % nt, 0)),
            pl.BlockSpec((None, 16, tm // 16, 3 * CHUNK_W), lambda i: (i // nt, 0, i % nt, 0)),
        ],
        out_shape=[
            jax.ShapeDtypeStruct((TOKENS, NAT_WIDTH), BF16),
            jax.ShapeDtypeStruct((N_SEQ, 4, SEQ // 4, 3 * CHUNK_W), BF16),
            jax.ShapeDtypeStruct((N_SEQ, 16, SEQ // 16, 3 * CHUNK_W), BF16),
        ],
        scratch_shapes=[pltpu.VMEM((tm, D_MODEL), BF16), pltpu.VMEM((tm, HEAD_DIM), F32)],
        compiler_params=pltpu.CompilerParams(
            dimension_semantics=("parallel",), vmem_limit_bytes=VMEM_LIMIT_BYTES),
        name="inproj",
    )(*xs, g, w, gain, rope)


def _mixer_a_kernel(sink_ref, q_ref, kp_ref, km_ref, kn_ref, vp_ref, vm_ref, vn_ref,
                    o_ref, kcat_ref, vcat_ref):
    kv = pl.program_id(1)
    t = pl.program_id(2)
    tq = q_ref.shape[0]
    blk = A_WINDOW
    kcat_ref[0:blk, :] = kp_ref[...]
    kcat_ref[blk:blk + tq, :] = km_ref[...]
    kcat_ref[blk + tq:, :] = kn_ref[...]
    vcat_ref[0:blk, :] = vp_ref[...]
    vcat_ref[blk:blk + tq, :] = vm_ref[...]
    vcat_ref[blk + tq:, :] = vn_ref[...]

    rows = A_GROUP * blk
    row = lax.broadcasted_iota(jnp.int32, (rows, 3 * blk), 0)
    col = lax.broadcasted_iota(jnp.int32, (rows, 3 * blk), 1)
    rel = (col - blk) - (row % blk)
    band = (rel <= A_WINDOW) & (rel >= -A_WINDOW)
    sink = jnp.concatenate(
        [jnp.full((blk, 1), sink_ref[kv * A_GROUP + g], F32) for g in range(A_GROUP)], axis=0)

    def body(sub, carry):
        q0 = pl.multiple_of(sub * blk, blk)
        q3 = q_ref[pl.ds(q0, blk), :]
        qs = jnp.concatenate([q3[:, g * HEAD_DIM:(g + 1) * HEAD_DIM] for g in range(A_GROUP)], axis=0)
        kw = kcat_ref[pl.ds(q0, 3 * blk), :]
        vw = vcat_ref[pl.ds(q0, 3 * blk), :]
        s = lax.dot_general(qs, kw, (((1,), (1,)), ((), ())), preferred_element_type=F32) * SCALE
        kbase = t * tq + q0 - blk
        valid = band & (col >= -kbase) & (col < SEQ - kbase)
        s = jnp.where(valid, s, NEG)
        m = jnp.maximum(jnp.max(s, axis=-1, keepdims=True), sink)
        p = jnp.exp(s - m)
        denom = jnp.sum(p, axis=-1, keepdims=True) + jnp.exp(sink - m)
        o = jnp.dot(p.astype(BF16), vw, preferred_element_type=F32) * (1.0 / denom)
        for g in range(A_GROUP):
            o_ref[pl.ds(q0, blk), g * HEAD_DIM:(g + 1) * HEAD_DIM] = o[g * blk:(g + 1) * blk].astype(BF16)
        return carry

    lax.fori_loop(0, tq // blk, body, 0, unroll=UNROLL_A)


def _mixer_a(sink, nat):
    tq = TQ_A
    nt = SEQ // tq
    nb = tq // A_WINDOW
    last = TOKENS // A_WINDOW - 1

    def main(c0):
        return lambda b, kv, t: (b * nt + t, c0 + kv)

    def prev(c0):
        return lambda b, kv, t: (jnp.maximum((b * nt + t) * nb - 1, 0), c0 + kv)

    def nxt(c0):
        return lambda b, kv, t: (jnp.minimum((b * nt + t + 1) * nb, last), c0 + kv)

    halo = (A_WINDOW, HEAD_DIM)
    return pl.pallas_call(
        _mixer_a_kernel,
        grid=(N_SEQ, A_KV_HEADS, nt),
        in_specs=[
            pl.BlockSpec(memory_space=pltpu.SMEM),
            pl.BlockSpec((tq, A_GROUP * HEAD_DIM), main(NAT_QA_384)),
            pl.BlockSpec(halo, prev(NAT_KA_128)),
            pl.BlockSpec((tq, HEAD_DIM), main(NAT_KA_128)),
            pl.BlockSpec(halo, nxt(NAT_KA_128)),
            pl.BlockSpec(halo, prev(NAT_VA_128)),
            pl.BlockSpec((tq, HEAD_DIM), main(NAT_VA_128)),
            pl.BlockSpec(halo, nxt(NAT_VA_128)),
        ],
        out_specs=pl.BlockSpec((tq, A_GROUP * HEAD_DIM), main(0)),
        out_shape=jax.ShapeDtypeStruct((TOKENS, A_Q_HEADS * HEAD_DIM), BF16),
        scratch_shapes=[pltpu.VMEM((tq + 2 * A_WINDOW, HEAD_DIM), BF16),
                        pltpu.VMEM((tq + 2 * A_WINDOW, HEAD_DIM), BF16)],
        compiler_params=pltpu.CompilerParams(
            dimension_semantics=("parallel", "parallel", "parallel"), vmem_limit_bytes=VMEM_LIMIT_BYTES),
        name="mixer_a",
    )(sink, nat, nat, nat, nat, nat, nat, nat)


def _mixer_b_kernel(q0_ref, k0p_ref, k0m_ref, k0n_ref, v0p_ref, v0m_ref, v0n_ref,
                    q1_ref, k1p_ref, k1m_ref, k1n_ref, v1p_ref, v1m_ref, v1n_ref,
                    q2_ref, k2p_ref, k2m_ref, k2n_ref, v2p_ref, v2m_ref, v2n_ref,
                    o_ref,
                    kb0_ref, vb0_ref, kb1_ref, vb1_ref, kb2_ref, vb2_ref,
                    onat_ref, lnat_ref, ores_ref, lres_ref):
    t = pl.program_id(1)
    tb = o_ref.shape[0]
    groups = (
        (1, q0_ref, (k0p_ref, k0m_ref, k0n_ref), (v0p_ref, v0m_ref, v0n_ref), kb0_ref, vb0_ref),
        (4, q1_ref, (k1p_ref, k1m_ref, k1n_ref), (v1p_ref, v1m_ref, v1n_ref), kb1_ref, vb1_ref),
        (16, q2_ref, (k2p_ref, k2m_ref, k2n_ref), (v2p_ref, v2m_ref, v2n_ref), kb2_ref, vb2_ref),
    )
    row = lax.broadcasted_iota(jnp.int32, (QB_B, QB_B + 2 * B_HALF), 0)
    col = lax.broadcasted_iota(jnp.int32, (QB_B, QB_B + 2 * B_HALF), 1)
    rel = (col - B_HALF) - row
    band = (rel <= B_HALF) & (rel >= -B_HALF)

    for gi, (d, q_ref, k_refs, v_refs, kb_ref, vb_ref) in enumerate(groups):
        p_len = tb // d
        sub_len = SEQ // d
        nsub = p_len // QB_B
        for src, dst in ((k_refs, kb_ref), (v_refs, vb_ref)):
            if d == 1:
                dst[0, 0:B_HALF, :] = src[0][...]
                dst[0, B_HALF:B_HALF + p_len, :] = src[1][...]
                dst[0, B_HALF + p_len:, :] = src[2][...]
            else:
                dst[:, 0:B_HALF, :] = src[0][...]
                dst[:, B_HALF:B_HALF + p_len, :] = src[1][...]
                dst[:, B_HALF + p_len:, :] = src[2][...]

        def body(it, carry, d=d, gi=gi, q_ref=q_ref, kb_ref=kb_ref, vb_ref=vb_ref,
                 p_len=p_len, sub_len=sub_len, nsub=nsub):
            r = it // nsub
            p0 = pl.multiple_of((it % nsub) * QB_B, QB_B)
            kbase = t * p_len + p0 - B_HALF
            valid = band & (col >= -kbase) & (col < sub_len - kbase)
            for h in range(2):
                cs = slice(h * HEAD_DIM, (h + 1) * HEAD_DIM)
                if d == 1:
                    q = q_ref[pl.ds(p0, QB_B), cs]
                else:
                    q = q_ref[r, pl.ds(p0, QB_B), cs]
                kw = kb_ref[r, pl.ds(p0, QB_B + 2 * B_HALF), cs]
                vw = vb_ref[r, pl.ds(p0, QB_B + 2 * B_HALF), cs]
                s = lax.dot_general(q, kw, (((1,), (1,)), ((), ())), preferred_element_type=F32) * SCALE
                s = jnp.where(valid, s, NEG)
                m = jnp.max(s, axis=-1, keepdims=True)
                p = jnp.exp(s - m)
                denom = jnp.sum(p, axis=-1, keepdims=True)
                o = jnp.dot(p.astype(BF16), vw, preferred_element_type=F32) * (1.0 / denom)
                lse = jnp.broadcast_to(m + jnp.log(denom), (QB_B, HEAD_DIM))
                if d == 1:
                    onat_ref[gi, h, pl.ds(p0, QB_B), :] = o
                    lnat_ref[gi, h, pl.ds(p0, QB_B), :] = lse
                else:
                    res0 = pl.multiple_of(r * p_len + p0, QB_B)
                    ores_ref[h, pl.ds(res0, QB_B), :] = o
                    lres_ref[h, pl.ds(res0, QB_B), :] = lse
            return carry

        lax.fori_loop(0, d * nsub, body, 0, unroll=UNROLL_B)
        if d > 1:
            for r in range(d):
                for h in range(2):
                    rows_nat = pl.ds(r, p_len, stride=d)
                    onat_ref[gi, h, rows_nat, :] = ores_ref[h, r * p_len:(r + 1) * p_len, :]
                    lnat_ref[gi, h, rows_nat, :] = lres_ref[h, r * p_len:(r + 1) * p_len, :]

    chunk = 256

    def combine(ci, carry):
        r0 = pl.multiple_of(ci * chunk, chunk)
        rows = pl.ds(r0, chunk)
        for h in range(2):
            l0, l1, l2 = lnat_ref[0, h, rows, :], lnat_ref[1, h, rows, :], lnat_ref[2, h, rows, :]
            mx = jnp.maximum(jnp.maximum(l0, l1), l2)
            e0, e1, e2 = jnp.exp(l0 - mx), jnp.exp(l1 - mx), jnp.exp(l2 - mx)
            inv = 1.0 / (e0 + e1 + e2)
            for gi, e in enumerate((e0, e1, e2)):
                lo = gi * CHUNK_W + h * HEAD_DIM
                o_ref[rows, lo:lo + HEAD_DIM] = (onat_ref[gi, h, rows, :] * (e * inv)).astype(BF16)
        return carry

    lax.fori_loop(0, tb // chunk, combine, 0)


def _mixer_b(nat, d4, d16):
    tb = TB_B
    nt = SEQ // tb
    in_specs = []
    args = []
    nh1 = tb // B_HALF
    last1 = TOKENS // B_HALF - 1
    in_specs.append(pl.BlockSpec((tb, CHUNK_W), lambda b, t: (b * nt + t, NAT_QB_256)))
    args.append(nat)
    for c0 in (NAT_KB_256, NAT_VB_256):
        in_specs += [
            pl.BlockSpec((B_HALF, CHUNK_W), lambda b, t, c0=c0: (jnp.maximum((b * nt + t) * nh1 - 1, 0), c0)),
            pl.BlockSpec((tb, CHUNK_W), lambda b, t, c0=c0: (b * nt + t, c0)),
            pl.BlockSpec((B_HALF, CHUNK_W), lambda b, t, c0=c0: (jnp.minimum((b * nt + t + 1) * nh1, last1), c0)),
        ]
        args += [nat, nat, nat]
    for d, arr in ((4, d4), (16, d16)):
        p_len = tb // d
        nh = p_len // B_HALF
        last = SEQ // d // B_HALF - 1
        in_specs.append(pl.BlockSpec((None, d, p_len, CHUNK_W), lambda b, t: (b, 0, t, 0)))
        args.append(arr)
        for c0 in (1, 2):
            in_specs += [
                pl.BlockSpec((None, d, B_HALF, CHUNK_W),
                             lambda b, t, c0=c0, nh=nh: (b, 0, jnp.maximum(t * nh - 1, 0), c0)),
                pl.BlockSpec((None, d, p_len, CHUNK_W), lambda b, t, c0=c0: (b, 0, t, c0)),
                pl.BlockSpec((None, d, B_HALF, CHUNK_W),
                             lambda b, t, c0=c0, nh=nh, last=last: (b, 0, jnp.minimum((t + 1) * nh, last), c0)),
            ]
            args += [arr, arr, arr]
    scratch = []
    for d in B_DILATIONS:
        shape = (d, tb // d + 2 * B_HALF, CHUNK_W)
        scratch += [pltpu.VMEM(shape, BF16), pltpu.VMEM(shape, BF16)]
    scratch += [pltpu.VMEM((3, 2, tb, HEAD_DIM), F32), pltpu.VMEM((3, 2, tb, HEAD_DIM), F32),
                pltpu.VMEM((2, tb, HEAD_DIM), F32), pltpu.VMEM((2, tb, HEAD_DIM), F32)]
    return pl.pallas_call(
        _mixer_b_kernel,
        grid=(N_SEQ, nt),
        in_specs=in_specs,
        out_specs=pl.BlockSpec((tb, B_HEADS * HEAD_DIM), lambda b, t: (b * nt + t, 0)),
        out_shape=jax.ShapeDtypeStruct((TOKENS, B_HEADS * HEAD_DIM), BF16),
        scratch_shapes=scratch,
        compiler_params=pltpu.CompilerParams(
            dimension_semantics=("parallel", "parallel"), vmem_limit_bytes=VMEM_LIMIT_BYTES),
        name="mixer_b",
    )(*args)


def _mixer_c_kernel(bias_ref, q_ref, kp_ref, km_ref, kn_ref, vp_ref, vm_ref, vn_ref,
                    o_ref, kcat_ref, vcat_ref):
    t = pl.program_id(1)
    tok = q_ref.shape[0]
    kcat_ref[0:tok, :] = kp_ref[...]
    kcat_ref[tok:2 * tok, :] = km_ref[...]
    kcat_ref[2 * tok:, :] = kn_ref[...]
    vcat_ref[0:tok, :] = vp_ref[...]
    vcat_ref[tok:2 * tok, :] = vm_ref[...]
    vcat_ref[2 * tok:, :] = vn_ref[...]

    nq = QROWS_C * GRID_W
    nk = KROWS_C * GRID_W
    row = lax.broadcasted_iota(jnp.int32, (nq, nk), 0)
    col = lax.broadcasted_iota(jnp.int32, (nq, nk), 1)
    qi, qc = row // GRID_W, row % GRID_W
    kj, kc = col // GRID_W, col % GRID_W
    cstart = jnp.clip(qc - NA_COLS // 2, 0, GRID_W - NA_COLS)
    col_valid = (kc >= cstart) & (kc < cstart + NA_COLS)
    for j in range(ROWS_C // QROWS_C):
        r = t * ROWS_C + j * QROWS_C
        first = jnp.clip(r + qi - NA_ROWS // 2, 0, GRID_ROWS - NA_ROWS)
        krel = (r - NA_ROWS // 2) + kj - first
        valid = col_valid & (krel >= 0) & (krel < NA_ROWS)
        k0 = (ROWS_C + j * QROWS_C - NA_ROWS // 2) * GRID_W
        for h in range(C_HEADS):
            cs = slice(h * HEAD_DIM, (h + 1) * HEAD_DIM)
            q = q_ref[j * nq:(j + 1) * nq, cs]
            kw = kcat_ref[k0:k0 + nk, cs]
            vw = vcat_ref[k0:k0 + nk, cs]
            s = lax.dot_general(q, kw, (((1,), (1,)), ((), ())), preferred_element_type=F32) * SCALE
            s = jnp.where(valid, s + bias_ref[h], NEG)
            m = jnp.max(s, axis=-1, keepdims=True)
            p = jnp.exp(s - m)
            denom = jnp.sum(p, axis=-1, keepdims=True)
            o = jnp.dot(p.astype(BF16), vw, preferred_element_type=F32) * (1.0 / denom)
            o_ref[j * nq:(j + 1) * nq, cs] = o.astype(BF16)


def _mixer_c(bias, nat):
    tok = TOK_C
    nt = SEQ // tok
    last = TOKENS // tok - 1
    width = C_HEADS * HEAD_DIM

    def main(c0):
        return lambda b, t: (b * nt + t, c0)

    def prev(c0):
        return lambda b, t: (jnp.maximum(b * nt + t - 1, 0), c0)

    def nxt(c0):
        return lambda b, t: (jnp.minimum(b * nt + t + 1, last), c0)

    blk = (tok, width)
    return pl.pallas_call(
        _mixer_c_kernel,
        grid=(N_SEQ, nt),
        in_specs=[
            pl.BlockSpec(bias.shape, lambda b, t: (0, 0, 0)),
            pl.BlockSpec(blk, main(NAT_QC_512)),
            pl.BlockSpec(blk, prev(NAT_KC_512)), pl.BlockSpec(blk, main(NAT_KC_512)), pl.BlockSpec(blk, nxt(NAT_KC_512)),
            pl.BlockSpec(blk, prev(NAT_VC_512)), pl.BlockSpec(blk, main(NAT_VC_512)), pl.BlockSpec(blk, nxt(NAT_VC_512)),
        ],
        out_specs=pl.BlockSpec(blk, main(0)),
        out_shape=jax.ShapeDtypeStruct((TOKENS, width), BF16),
        scratch_shapes=[pltpu.VMEM((3 * tok, width), BF16), pltpu.VMEM((3 * tok, width), BF16)],
        compiler_params=pltpu.CompilerParams(
            dimension_semantics=("parallel", "parallel"), vmem_limit_bytes=VMEM_LIMIT_BYTES),
        name="mixer_c",
    )(bias, nat, nat, nat, nat, nat, nat, nat)


def _outproj_kernel(bounds, *refs):
    nx = len(bounds)
    x_refs = refs[:nx]
    oa_ref, ob_ref, oc_ref, w_ref, g_ref, x1_ref, hn_ref, mixed_ref = refs[nx:]
    wa = oa_ref.shape[1]
    wb = ob_ref.shape[1]
    mixed_ref[:, 0:wa] = oa_ref[...]
    mixed_ref[:, wa:wa + wb] = ob_ref[...]
    mixed_ref[:, wa + wb:] = oc_ref[...]
    if nx == 1:
        res_ref = x_refs[0]
    else:
        def stage(x_ref):
            x1_ref[...] = x_ref[...]

        _for_owner(bounds, x_refs, stage)
        res_ref = x1_ref
    nc = 512
    ss = jnp.zeros((x1_ref.shape[0], 1), F32)
    for c in range(D_MODEL // nc):
        cs = slice(c * nc, (c + 1) * nc)
        y = res_ref[:, cs] + jnp.dot(mixed_ref[...], w_ref[:, cs], preferred_element_type=F32)
        x1_ref[:, cs] = y
        ss = ss + jnp.sum(y * y, axis=-1, keepdims=True)
    inv = lax.rsqrt(ss * (1.0 / D_MODEL) + EPS)
    hn_ref[...] = (x1_ref[...] * inv * g_ref[...]).astype(BF16)


def _outproj(xs, oa, ob, oc, w, g):
    tm = TM_PROJ
    const = lambda i: (0, 0)
    row = lambda i: (i, 0)
    bounds = _row_tiles(xs, tm)
    return pl.pallas_call(
        functools.partial(_outproj_kernel, bounds),
        grid=(TOKENS // tm,),
        in_specs=[_stacked_spec(b, tm, D_MODEL, 1) for b in bounds] + [
            pl.BlockSpec((tm, oa.shape[1]), row),
            pl.BlockSpec((tm, ob.shape[1]), row),
            pl.BlockSpec((tm, oc.shape[1]), row),
            pl.BlockSpec((MIX_WIDTH, D_MODEL), const, pipeline_mode=pl.Buffered(1)),
            pl.BlockSpec((1, D_MODEL), const),
        ],
        out_specs=[pl.BlockSpec((tm, D_MODEL), row), pl.BlockSpec((tm, D_MODEL), row)],
        out_shape=[jax.ShapeDtypeStruct((TOKENS, D_MODEL), F32),
                   jax.ShapeDtypeStruct((TOKENS, D_MODEL), BF16)],
        scratch_shapes=[pltpu.VMEM((tm, MIX_WIDTH), BF16)],
        compiler_params=pltpu.CompilerParams(
            dimension_semantics=("parallel",), vmem_limit_bytes=VMEM_LIMIT_BYTES),
        name="outproj",
    )(*xs, oa, ob, oc, w, g)


def _ffn_kernel(bounds, hn_ref, x1_ref, wg_ref, wu_ref, wd_ref, *refs):
    o_refs = refs[:len(bounds)]
    acc_ref = o_refs[0] if len(bounds) == 1 else refs[len(bounds)]
    f = pl.program_id(1)

    @pl.when(f == 0)
    def _():
        acc_ref[...] = x1_ref[...]

    hn = hn_ref[...]
    g = jnp.dot(hn, wg_ref[...], preferred_element_type=F32)
    u = jnp.dot(hn, wu_ref[...], preferred_element_type=F32)
    h = (g * jax.nn.sigmoid(g) * u).astype(BF16)
    nc = 512
    for c in range(D_MODEL // nc):
        cs = slice(c * nc, (c + 1) * nc)
        acc_ref[:, cs] += jnp.dot(h, wd_ref[:, cs], preferred_element_type=F32)

    if len(bounds) > 1:
        def emit(o_ref):
            o_ref[...] = acc_ref[...]

        @pl.when(f == pl.num_programs(1) - 1)
        def _():
            _for_owner(bounds, o_refs, emit)


def _ffn(hn, x1, wg, wu, wd, out_rows):
    tm, tf = TM_FFN, TF_FFN
    row = lambda i, f: (i, 0)
    outs = [jax.ShapeDtypeStruct((r, D_MODEL), F32) for r in out_rows]
    bounds = _row_tiles(outs, tm)
    return pl.pallas_call(
        functools.partial(_ffn_kernel, bounds),
        grid=(TOKENS // tm, D_FF // tf),
        in_specs=[
            pl.BlockSpec((tm, D_MODEL), row),
            pl.BlockSpec((tm, D_MODEL), row),
            pl.BlockSpec((D_MODEL, tf), lambda i, f: (0, f)),
            pl.BlockSpec((D_MODEL, tf), lambda i, f: (0, f)),
            pl.BlockSpec((tf, D_MODEL), lambda i, f: (f, 0)),
        ],
        out_specs=[_stacked_spec(b, tm, D_MODEL, 2) for b in bounds],
        out_shape=outs,
        scratch_shapes=[pltpu.VMEM((tm, D_MODEL), F32)] if len(outs) > 1 else [],
        compiler_params=pltpu.CompilerParams(
            dimension_semantics=("arbitrary", "arbitrary"), vmem_limit_bytes=VMEM_LIMIT_BYTES),
        name="ffn",
    )(hn, x1, wg, wu, wd)


def _rope_tables():
    inv = jnp.asarray((ROPE_THETA ** (-np.arange(0, ROT_DIM, 2, dtype=np.float32) / ROT_DIM)).astype(np.float32))
    ang = jnp.arange(SEQ, dtype=F32)[:, None] * inv[None, :]
    cos, sin = jnp.cos(ang), jnp.sin(ang)
    return jnp.concatenate([cos, cos, -sin, sin, jnp.zeros((SEQ, HEAD_DIM - 2 * ROT_DIM), F32)], axis=1)


def _neighbourhood_bias(rpb):
    pad = GRID_W - NA_COLS
    p = jnp.pad(rpb.astype(F32), ((0, 0), (0, 0), (pad, pad)), mode="edge")
    cols = jnp.stack([p[:, :, GRID_W - 1 - c:2 * GRID_W - 1 - c] for c in range(GRID_W)], axis=2)
    lo = NA_ROWS // 2 - 1
    b = jnp.stack([cols[:, lo - qi:lo - qi + KROWS_C] for qi in range(QROWS_C)], axis=1)
    return jnp.transpose(b, (0, 1, 3, 2, 4)).reshape(C_HEADS, QROWS_C * GRID_W, KROWS_C * GRID_W)


def kernel(x_prompt, x_sample, norm_mix, w_in, qk_norm, sink_a, rpb_c, w_out, norm_ffn, w_gate, w_up, w_down):
    xs = (x_prompt.reshape(-1, D_MODEL), x_sample.reshape(-1, D_MODEL))
    out_rows = [(TOKENS,)] * (DEPTH - 1) + [tuple(x.shape[0] for x in xs)]
    rope = _rope_tables()
    for l in range(DEPTH):
        nat, d4, d16 = _inproj(xs, norm_mix[l][None, :], w_in[l].astype(BF16),
                               qk_norm[l].reshape(6, HEAD_DIM).astype(F32), rope)
        oa = _mixer_a(sink_a[l].astype(F32), nat)
        ob = _mixer_b(nat, d4, d16)
        oc = _mixer_c(_neighbourhood_bias(rpb_c[l]), nat)
        x1, hn = _outproj(xs, oa, ob, oc, w_out[l].astype(BF16), norm_ffn[l][None, :])
        xs = tuple(_ffn(hn, x1, w_gate[l].astype(BF16), w_up[l].astype(BF16), w_down[l].astype(BF16),
                        out_rows[l]))
    return (xs[0].reshape(x_prompt.shape), xs[1].reshape(x_sample.shape))
```
